```python
import math
import jax, jax.numpy as jnp
from jax import lax
import numpy as np

D_MODEL = 1024
BATCH = 8
SEQ = 8192
DEPTH = 4

N_MIXERS = 4
MIX_WIDTH = D_MODEL
GROUP_WIDTH = MIX_WIDTH // N_MIXERS
SC_HEAD_DIM = 64
SC_TAPS = 3
POOL_WINDOWS = (2, 4, 8, 16)
POOL_GROUP = GROUP_WIDTH // len(POOL_WINDOWS)
CF_TAPS = 31
SSM_CH = 16
SSM_GROUPS = GROUP_WIDTH // SSM_CH
SSM_STATE = 64
SSM_DT_MIN = 1e-3
SSM_DT_MAX = 1e-1
D_FF = -(-(8 * D_MODEL) // (3 * 256)) * 256
IN_SIZES = (GROUP_WIDTH, GROUP_WIDTH, GROUP_WIDTH, GROUP_WIDTH, GROUP_WIDTH, GROUP_WIDTH, GROUP_WIDTH)
IN_WIDTH = sum(IN_SIZES)
IN_SPLITS = tuple(int(v) for v in np.cumsum(IN_SIZES)[:-1])
ALPHA = (2 * DEPTH) ** 0.25
BETA = (8 * DEPTH) ** -0.25
LN_EPS = 1e-5

kernel_name = "hybrid_parallel_mixer_trunk"


def layer_norm(x, g, b):
    xf = x.astype(jnp.float32)
    mu = jnp.mean(xf, axis=-1, keepdims=True)
    var = jnp.mean(jnp.square(xf - mu), axis=-1, keepdims=True)
    y = (xf - mu) * lax.rsqrt(var + LN_EPS) * g.astype(jnp.float32) + b.astype(jnp.float32)
    return y.astype(x.dtype)


def causal_dwconv(x, w):
    k, ch = w.shape
    return lax.conv_general_dilated(
        x, w[:, None, :].astype(x.dtype), window_strides=(1,), padding=[(k - 1, 0)],
        dimension_numbers=('NWC', 'WIO', 'NWC'), feature_group_count=ch)


def pool_mixer(z, w_pool, scale):
    zf = z.astype(jnp.float32)
    s = z.shape[1]
    pos = jnp.arange(1, s + 1, dtype=jnp.float32)[None, :, None]
    outs = []
    for i, w in enumerate(POOL_WINDOWS):
        zg = zf[..., i * POOL_GROUP:(i + 1) * POOL_GROUP]
        cs = jnp.cumsum(zg, axis=1)
        lagged = jnp.pad(cs, ((0, 0), (w, 0), (0, 0)))[:, :s]
        mean = (cs - lagged) / jnp.minimum(pos, float(w))
        outs.append(jnp.einsum('bsc,cd->bsd', mean - zg, w_pool[i].astype(jnp.float32)))
    return (jnp.concatenate(outs, axis=-1) * scale.astype(jnp.float32)).astype(z.dtype)


def conformer_conv(z, dw_w, dw_b, ln_g, ln_b):
    val, gate = jnp.split(z, 2, axis=-1)
    h = val * jax.nn.sigmoid(gate)
    h = causal_dwconv(h, dw_w) + dw_b
    h = layer_norm(h, ln_g, ln_b)
    return jax.nn.silu(h)


def ssm_mixer(u, lam_re, lam_im, log_dt, b_re, b_im, c_re, c_im, d, w_glu, b_glu):
    bsz, s, _ = u.shape
    f32 = jnp.float32
    uf = u.astype(f32).reshape(bsz, s, SSM_GROUPS, SSM_CH)
    lam = lax.complex(lam_re.astype(f32), lam_im.astype(f32))
    dt = jnp.exp(log_dt.astype(f32))[:, None]
    lam_bar = jnp.exp(lam * dt)
    bmat = lax.complex(b_re.astype(f32), b_im.astype(f32))
    b_bar = ((lam_bar - 1.0) / lam)[..., None] * bmat
    bu = jnp.einsum('gph,bsgh->bsgp', b_bar, uf)
    a = jnp.broadcast_to(lam_bar, (1, s) + lam_bar.shape)

    def combine(e1, e2):
        a1, b1 = e1
        a2, b2 = e2
        return a1 * a2, a2 * b1 + b2

    _, states = lax.associative_scan(combine, (a, bu), axis=1)
    cmat = lax.complex(c_re.astype(f32), c_im.astype(f32))
    y = jnp.real(jnp.einsum('ghp,bsgp->bsgh', cmat, states)) + d.astype(f32).reshape(SSM_GROUPS, SSM_CH) * uf
    y = y.reshape(bsz, s, GROUP_WIDTH)
    yg = jax.nn.gelu(y)
    out = yg * jax.nn.sigmoid(yg @ w_glu.astype(f32) + b_glu.astype(f32))
    return out.astype(u.dtype)


def setup_inputs(seed: int = 0) -> dict:
    key = jax.random.key(seed)
    ks = jax.random.split(key, 32)
    L = DEPTH
    f32 = jnp.float32

    def nrm(k, shape, scale):
        return jax.random.normal(k, shape, f32) * scale

    lam_im_base = jnp.pi * jnp.arange(SSM_STATE, dtype=f32)
    return {
        "x": nrm(ks[0], (BATCH, SEQ, D_MODEL), 1.0),
        "c": nrm(ks[1], (BATCH, D_MODEL), 1.0),
        "w_ada": nrm(ks[2], (L, D_MODEL, 6 * D_MODEL), 0.1 * D_MODEL ** -0.5),
        "b_ada": nrm(ks[3], (L, 6 * D_MODEL), 0.01),
        "w_in": nrm(ks[4], (L, D_MODEL, IN_WIDTH), D_MODEL ** -0.5),
        "b_in": nrm(ks[5], (L, IN_WIDTH), 0.01),
        "sc_w": nrm(ks[6], (L, SC_TAPS, GROUP_WIDTH), SC_TAPS ** -0.5),
        "pool_w": nrm(ks[7], (L, len(POOL_WINDOWS), POOL_GROUP, POOL_GROUP), POOL_GROUP ** -0.5),
        "pool_scale": 1.0 + nrm(ks[8], (L, GROUP_WIDTH), 0.02),
        "cf_dw_w": nrm(ks[9], (L, CF_TAPS, GROUP_WIDTH), CF_TAPS ** -0.5),
        "cf_dw_b": nrm(ks[10], (L, GROUP_WIDTH), 0.01),
        "cf_ln_g": 1.0 + nrm(ks[11], (L, GROUP_WIDTH), 0.02),
        "cf_ln_b": nrm(ks[12], (L, GROUP_WIDTH), 0.01),
        "ssm_lam_re": -0.5 + nrm(ks[13], (L, SSM_GROUPS, SSM_STATE), 0.01),
        "ssm_lam_im": lam_im_base + nrm(ks[14], (L, SSM_GROUPS, SSM_STATE), 0.01),
        "ssm_log_dt": jax.random.uniform(ks[15], (L, SSM_GROUPS), f32, math.log(SSM_DT_MIN), math.log(SSM_DT_MAX)),
        "ssm_b_re": nrm(ks[16], (L, SSM_GROUPS, SSM_STATE, SSM_CH), (2 * SSM_CH) ** -0.5),
        "ssm_b_im": nrm(ks[17], (L, SSM_GROUPS, SSM_STATE, SSM_CH), (2 * SSM_CH) ** -0.5),
        "ssm_c_re": nrm(ks[18], (L, SSM_GROUPS, SSM_CH, SSM_STATE), SSM_STATE ** -0.5),
        "ssm_c_im": nrm(ks[19], (L, SSM_GROUPS, SSM_CH, SSM_STATE), SSM_STATE ** -0.5),
        "ssm_d": nrm(ks[20], (L, GROUP_WIDTH), 1.0),
        "ssm_w_glu": nrm(ks[21], (L, GROUP_WIDTH, GROUP_WIDTH), GROUP_WIDTH ** -0.5),
        "ssm_b_glu": nrm(ks[22], (L, GROUP_WIDTH), 0.01),
        "w_o": nrm(ks[23], (L, MIX_WIDTH, D_MODEL), BETA * MIX_WIDTH ** -0.5),
        "ln1_g": 1.0 + nrm(ks[24], (L, D_MODEL), 0.02),
        "ln1_b": nrm(ks[25], (L, D_MODEL), 0.01),
        "w_gate": nrm(ks[26], (L, D_MODEL, D_FF), D_MODEL ** -0.5),
        "w_up": nrm(ks[27], (L, D_MODEL, D_FF), D_MODEL ** -0.5),
        "w_down": nrm(ks[28], (L, D_FF, D_MODEL), BETA * D_FF ** -0.5),
        "ln2_g": 1.0 + nrm(ks[29], (L, D_MODEL), 0.02),
        "ln2_b": nrm(ks[30], (L, D_MODEL), 0.01),
    }


def reference(x, c, w_ada, b_ada, w_in, b_in, sc_w, pool_w, pool_scale, cf_dw_w, cf_dw_b, cf_ln_g, cf_ln_b,
              ssm_lam_re, ssm_lam_im, ssm_log_dt, ssm_b_re, ssm_b_im, ssm_c_re, ssm_c_im, ssm_d,
              ssm_w_glu, ssm_b_glu, w_o, ln1_g, ln1_b, w_gate, w_up, w_down, ln2_g, ln2_b):
    cond = jax.nn.silu(c)
    for l in range(DEPTH):
        mod = cond @ w_ada[l] + b_ada[l]
        sh1, sc1, g1, sh2, sc2, g2 = [m[:, None, :] for m in jnp.split(mod, 6, axis=-1)]

        h = x * (1 + sc1) + sh1
        z = h @ w_in[l] + b_in[l]
        z_h, z_b, z_c, z_p, z_cv, z_cg, z_s = jnp.split(z, IN_SPLITS, axis=-1)
        y_a = z_b * causal_dwconv(z_c * z_h, sc_w[l])
        y_b = pool_mixer(z_p, pool_w[l], pool_scale[l])
        y_c = conformer_conv(jnp.concatenate([z_cv, z_cg], axis=-1),
                             cf_dw_w[l], cf_dw_b[l], cf_ln_g[l], cf_ln_b[l])
        y_d = ssm_mixer(z_s, ssm_lam_re[l], ssm_lam_im[l], ssm_log_dt[l], ssm_b_re[l], ssm_b_im[l],
                        ssm_c_re[l], ssm_c_im[l], ssm_d[l], ssm_w_glu[l], ssm_b_glu[l])
        y = jnp.concatenate([y_a, y_b, y_c, y_d], axis=-1) @ w_o[l]
        x = layer_norm(ALPHA * x + (1 + g1) * y, ln1_g[l], ln1_b[l])

        h = x * (1 + sc2) + sh2
        f = (jax.nn.silu(h @ w_gate[l]) * (h @ w_up[l])) @ w_down[l]
        x = layer_norm(ALPHA * x + (1 + g2) * f, ln2_g[l], ln2_b[l])
    return x
```

```python
import functools
import math

import jax
import jax.numpy as jnp
from jax import lax
from jax.experimental import pallas as pl
from jax.experimental.pallas import tpu as pltpu

F32 = jnp.float32
BF16 = jnp.bfloat16

N_MIXERS = 4
SC_TAPS = 3
POOL_WINDOWS = (2, 4, 8, 16)
CF_TAPS = 31
SSM_CH = 16
SSM_STATE = 64
LN_EPS = 1e-5

SUBLANES = 8
VMEM_LIMIT_BYTES = 56 * 1024 * 1024

MIX_STEPS = 64
FFN_STEPS = 64
HIST_STEPS = 32
CONV_ROWS = 32
FFN_CHUNK = 256
MOD_TN = 1536


def _sigmoid(v):
    return 1.0 / (1.0 + jnp.exp(-v))


def _layer_norm(v, g, b):
    mu = jnp.mean(v, axis=-1, keepdims=True)
    c = v - mu
    var = jnp.mean(c * c, axis=-1, keepdims=True)
    return c * lax.rsqrt(var + LN_EPS) * g + b


def _per_batch(v, m):
    rows, d = v.shape
    return jnp.broadcast_to(m[None], (rows // SUBLANES, SUBLANES, d)).reshape(rows, d)


def _mod_kernel(c_ref, w_ref, b_ref, o_ref):
    c = c_ref[...]
    cond = c * _sigmoid(c)
    o_ref[0] = jnp.dot(cond.astype(BF16), w_ref[0].astype(BF16), preferred_element_type=F32) + b_ref[0]


def _modulation(c, w_ada, b_ada):
    depth, d, n = w_ada.shape
    bsz = c.shape[0]
    return pl.pallas_call(
        _mod_kernel,
        grid=(depth, n // MOD_TN),
        in_specs=[
            pl.BlockSpec((bsz, d), lambda l, j: (0, 0)),
            pl.BlockSpec((1, d, MOD_TN), lambda l, j: (l, 0, j)),
            pl.BlockSpec((1, 1, MOD_TN), lambda l, j: (l, 0, j)),
        ],
        out_specs=pl.BlockSpec((1, bsz, MOD_TN), lambda l, j: (l, 0, j)),
        out_shape=jax.ShapeDtypeStruct((depth, bsz, n), F32),
        compiler_params=pltpu.CompilerParams(
            dimension_semantics=("arbitrary", "arbitrary"), vmem_limit_bytes=VMEM_LIMIT_BYTES),
        name="adaln_mod",
    )(c, w_ada, b_ada.reshape(depth, 1, n))


def _mixer_kernel(x_ref, mod_ref, w_in_ref, b_in_ref, sc_w_ref, wp_ref, pscale_ref, cfw_ref, cfb_ref,
                  cfg_ref, cfbeta_ref, lre_ref, lim_ref, bcat_ref, ccat_ref, dskip_ref, wglu_ref,
                  bglu_ref, wo_ref, ln_g_ref, ln_b_ref, o_ref,
                  hist_ref, bu_ref, xs_ref, carry_ref, ycat_ref, *, alpha):
    rows, d = x_ref.shape
    steps = rows // SUBLANES
    gw = d // N_MIXERS
    hrows = HIST_STEPS * SUBLANES
    n_state = lre_ref.shape[1]
    i = pl.program_id(0)

    @pl.when(i == 0)
    def _():
        hist_ref[0:hrows, :] = jnp.zeros((hrows, hist_ref.shape[1]), F32)
        carry_ref[...] = jnp.zeros(carry_ref.shape, F32)

    @pl.when(i > 0)
    def _():
        hist_ref[0:hrows, :] = hist_ref[rows:rows + hrows, :]

    x = x_ref[...]
    sh1 = mod_ref[:, 0:d]
    sc1 = mod_ref[:, d:2 * d]
    g1 = mod_ref[:, 2 * d:3 * d]
    hb = (x * _per_batch(x, 1.0 + sc1) + _per_batch(x, sh1)).astype(BF16)

    def proj(c0, c1):
        return jnp.dot(hb, w_in_ref[:, c0:c1], preferred_element_type=F32) + b_in_ref[:, c0:c1]

    z3 = proj(0, 3 * gw)
    z_h, z_b, z_c = z3[:, 0:gw], z3[:, gw:2 * gw], z3[:, 2 * gw:3 * gw]
    p = z_c * z_h
    hist_ref[hrows:hrows + rows, 0:gw] = p
    conv = sc_w_ref[SC_TAPS - 1:SC_TAPS, :] * p
    for k in range(SC_TAPS - 1):
        off = hrows - SUBLANES * (SC_TAPS - 1 - k)
        conv = conv + sc_w_ref[k:k + 1, :] * hist_ref[off:off + rows, 0:gw]
    ycat_ref[:, 0:gw] = (z_b * conv).astype(BF16)

    z_p = proj(3 * gw, 4 * gw)
    hist_ref[hrows:hrows + rows, gw:2 * gw] = z_p
    max_w = max(POOL_WINDOWS)
    back = SUBLANES * (max_w - 1)
    ext = hist_ref[hrows - back:hrows + rows, gw:2 * gw]
    lane = lax.broadcasted_iota(jnp.int32, (rows, gw), 1)
    pool_group = gw // len(POOL_WINDOWS)
    t_glob = i * steps + lax.broadcasted_iota(jnp.int32, (rows, gw), 0) // SUBLANES
    run = ext
    width = 1
    s_sel = None
    w_lane = None
    for gi, w in enumerate(POOL_WINDOWS):
        while width < w:
            sh = SUBLANES * width
            run = run[sh:] + run[:-sh]
            width *= 2
        cur = run[run.shape[0] - rows:]
        if s_sel is None:
            s_sel, w_lane = cur, jnp.full((rows, gw), float(w), F32)
        else:
            in_group = lane >= gi * pool_group
            s_sel = jnp.where(in_group, cur, s_sel)
            w_lane = jnp.where(in_group, float(w), w_lane)
    cnt = jnp.minimum((t_glob + 1).astype(F32), w_lane)
    pooled = s_sel / cnt - z_p
    y_b = jnp.dot(pooled.astype(BF16), wp_ref[...], preferred_element_type=F32) * pscale_ref[...]
    ycat_ref[:, gw:2 * gw] = y_b.astype(BF16)

    z_cf = proj(4 * gw, 6 * gw)
    hist_ref[hrows:hrows + rows, 2 * gw:3 * gw] = z_cf[:, 0:gw] * _sigmoid(z_cf[:, gw:2 * gw])

    def conv_block(j, carry):
        r0 = pl.multiple_of(j * CONV_ROWS, CONV_ROWS)
        acc = jnp.broadcast_to(cfb_ref[...], (CONV_ROWS, gw))
        for k in range(CF_TAPS):
            off = hrows - SUBLANES * (CF_TAPS - 1 - k)
            acc = acc + cfw_ref[k:k + 1, :] * hist_ref[pl.ds(r0 + off, CONV_ROWS), 2 * gw:3 * gw]
        hn = _layer_norm(acc, cfg_ref[...], cfbeta_ref[...])
        ycat_ref[pl.ds(r0, CONV_ROWS), 2 * gw:3 * gw] = (hn * _sigmoid(hn)).astype(BF16)
        return carry

    lax.fori_loop(0, rows // CONV_ROWS, conv_block, 0)

    u = proj(6 * gw, 7 * gw)
    bu_ref[...] = jnp.dot(u.astype(BF16), bcat_ref[...], preferred_element_type=F32)
    a_re = lre_ref[...]
    a_im = lim_ref[...]

    def scan_pair(s2, carry):
        re, im = carry
        r0 = pl.multiple_of(s2 * 2 * SUBLANES, 2 * SUBLANES)
        res, ims = [], []
        for q in range(2):
            b_re = bu_ref[pl.ds(r0 + q * SUBLANES, SUBLANES), 0:n_state]
            b_im = bu_ref[pl.ds(r0 + q * SUBLANES, SUBLANES), n_state:2 * n_state]
            re, im = a_re * re - a_im * im + b_re, a_re * im + a_im * re + b_im
            res.append(re)
            ims.append(im)
        xs_ref[pl.ds(r0, 2 * SUBLANES), 0:n_state] = jnp.concatenate(res, axis=0).astype(BF16)
        xs_ref[pl.ds(r0, 2 * SUBLANES), n_state:2 * n_state] = jnp.concatenate(ims, axis=0).astype(BF16)
        return re, im

    re, im = lax.fori_loop(0, steps // 2, scan_pair,
                           (carry_ref[:, 0:n_state], carry_ref[:, n_state:2 * n_state]))
    carry_ref[:, 0:n_state] = re
    carry_ref[:, n_state:2 * n_state] = im
    y_s = jnp.dot(xs_ref[...], ccat_ref[...], preferred_element_type=F32) + dskip_ref[...] * u
    y_g = jax.nn.gelu(y_s)
    gate = jnp.dot(y_g.astype(BF16), wglu_ref[...], preferred_element_type=F32) + bglu_ref[...]
    ycat_ref[:, 3 * gw:4 * gw] = (y_g * _sigmoid(gate)).astype(BF16)

    y = jnp.dot(ycat_ref[...], wo_ref[...], preferred_element_type=F32)
    o_ref[...] = _layer_norm(alpha * x + _per_batch(x, 1.0 + g1) * y, ln_g_ref[...], ln_b_ref[...])


def _const_spec(shape):
    nd = len(shape)
    return pl.BlockSpec(shape, lambda i: (0,) * nd)


def _mixer_layer(xt, mod_l, wts, *, alpha):
    n_rows, d = xt.shape
    rows = MIX_STEPS * SUBLANES
    hrows = HIST_STEPS * SUBLANES
    gw = d // N_MIXERS
    n_state2 = wts["bcat"].shape[1]
    weights = [wts[k] for k in ("w_in", "b_in", "sc_w", "wp", "pscale", "cfw", "cfb", "cfg", "cfbeta",
                                "lre", "lim", "bcat", "ccat", "dskip", "wglu", "bglu", "wo", "ln1_g", "ln1_b")]
    row_spec = pl.BlockSpec((rows, d), lambda i: (i, 0))
    return pl.pallas_call(
        functools.partial(_mixer_kernel, alpha=alpha),
        grid=(n_rows // rows,),
        in_specs=[row_spec, _const_spec(mod_l.shape)] + [_const_spec(w.shape) for w in weights],
        out_specs=row_spec,
        out_shape=jax.ShapeDtypeStruct((n_rows, d), F32),
        scratch_shapes=[
            pltpu.VMEM((hrows + rows, 3 * gw), F32),
            pltpu.VMEM((rows, n_state2), F32),
            pltpu.VMEM((rows, n_state2), BF16),
            pltpu.VMEM((SUBLANES, n_state2), F32),
            pltpu.VMEM((rows, d), BF16),
        ],
        compiler_params=pltpu.CompilerParams(
            dimension_semantics=("arbitrary",), vmem_limit_bytes=VMEM_LIMIT_BYTES),
        name="mixer_layer",
    )(xt, mod_l, *weights)


def _ffn_kernel(x_ref, mod_ref, wg_ref, wu_ref, wd_ref, ln_g_ref, ln_b_ref, o_ref, hid_ref, *, alpha):
    rows, d = x_ref.shape
    d_ff = wg_ref.shape[1]
    x = x_ref[...]
    sh2 = mod_ref[:, 3 * d:4 * d]
    sc2 = mod_ref[:, 4 * d:5 * d]
    g2 = mod_ref[:, 5 * d:6 * d]
    hb = (x * _per_batch(x, 1.0 + sc2) + _per_batch(x, sh2)).astype(BF16)
    for c0 in range(0, d_ff, FFN_CHUNK):
        g = jnp.dot(hb, wg_ref[:, c0:c0 + FFN_CHUNK], preferred_element_type=F32)
        u = jnp.dot(hb, wu_ref[:, c0:c0 + FFN_CHUNK], preferred_element_type=F32)
        hid_ref[:, c0:c0 + FFN_CHUNK] = (g * _sigmoid(g) * u).astype(BF16)
    f = jnp.dot(hid_ref[...], wd_ref[...], preferred_element_type=F32)
    o_ref[...] = _layer_norm(alpha * x + _per_batch(x, 1.0 + g2) * f, ln_g_ref[...], ln_b_ref[...])


def _ffn_layer(xt, mod_l, wg, wu, wd, ln_g, ln_b, *, alpha):
    n_rows, d = xt.shape
    rows = FFN_STEPS * SUBLANES
    d_ff = wg.shape[1]
    row_spec = pl.BlockSpec((rows, d), lambda i: (i, 0))
    weights = [wg, wu, wd, ln_g, ln_b]
    return pl.pallas_call(
        functools.partial(_ffn_kernel, alpha=alpha),
        grid=(n_rows // rows,),
        in_specs=[row_spec, _const_spec(mod_l.shape)] + [_const_spec(w.shape) for w in weights],
        out_specs=row_spec,
        out_shape=jax.ShapeDtypeStruct((n_rows, d), F32),
        scratch_shapes=[pltpu.VMEM((rows, d_ff), BF16)],
        compiler_params=pltpu.CompilerParams(
            dimension_semantics=("arbitrary",), vmem_limit_bytes=VMEM_LIMIT_BYTES),
        name="ffn_layer",
    )(xt, mod_l, *weights)


def _block_diag(blocks):
    depth, n, a, b = blocks.shape
    eye = jnp.eye(n, dtype=blocks.dtype)
    return jnp.einsum("lnab,nm->lnamb", blocks, eye).reshape(depth, n * a, n * b)


def _ssm_tables(lam_re, lam_im, log_dt, b_re, b_im, c_re, c_im):
    depth, groups, n_p = lam_re.shape
    dt = jnp.exp(log_dt)[..., None]
    mag = jnp.exp(lam_re * dt)
    bar_re = mag * jnp.cos(lam_im * dt)
    bar_im = mag * jnp.sin(lam_im * dt)
    nr, ni = bar_re - 1.0, bar_im
    den = lam_re * lam_re + lam_im * lam_im
    q_re = (nr * lam_re + ni * lam_im) / den
    q_im = (ni * lam_re - nr * lam_im) / den
    bb_re = q_re[..., None] * b_re - q_im[..., None] * b_im
    bb_im = q_re[..., None] * b_im + q_im[..., None] * b_re
    bcat = jnp.concatenate([_block_diag(jnp.swapaxes(bb_re, 2, 3)), _block_diag(jnp.swapaxes(bb_im, 2, 3))],
                           axis=2)
    ccat = jnp.concatenate([_block_diag(jnp.swapaxes(c_re, 2, 3)), -_block_diag(jnp.swapaxes(c_im, 2, 3))],
                           axis=1)
    flat = (depth, 1, groups * n_p)
    lre = jnp.broadcast_to(bar_re.reshape(flat), (depth, SUBLANES, groups * n_p))
    lim = jnp.broadcast_to(bar_im.reshape(flat), (depth, SUBLANES, groups * n_p))
    return lre, lim, bcat.astype(BF16), ccat.astype(BF16)


def kernel(x, c, w_ada, b_ada, w_in, b_in, sc_w, pool_w, pool_scale, cf_dw_w, cf_dw_b, cf_ln_g, cf_ln_b,
           ssm_lam_re, ssm_lam_im, ssm_log_dt, ssm_b_re, ssm_b_im, ssm_c_re, ssm_c_im, ssm_d,
           ssm_w_glu, ssm_b_glu, w_o, ln1_g, ln1_b, w_gate, w_up, w_down, ln2_g, ln2_b):
    bsz, seq, d = x.shape
    depth = w_in.shape[0]
    assert bsz == SUBLANES, "layout puts the batch on the 8 sublanes"
    assert seq % MIX_STEPS == 0 and seq % FFN_STEPS == 0 and HIST_STEPS >= CF_TAPS - 1
    alpha = (2 * depth) ** 0.25

    mod = _modulation(c, w_ada, b_ada)
    lre, lim, bcat, ccat = _ssm_tables(ssm_lam_re, ssm_lam_im, ssm_log_dt, ssm_b_re, ssm_b_im,
                                       ssm_c_re, ssm_c_im)
    wp = _block_diag(pool_w).astype(BF16)
    row = lambda a: a.reshape(depth, 1, a.shape[-1])
    w_in_b, w_o_b, w_glu_b = w_in.astype(BF16), w_o.astype(BF16), ssm_w_glu.astype(BF16)
    wg_b, wu_b, wd_b = w_gate.astype(BF16), w_up.astype(BF16), w_down.astype(BF16)

    xt = jnp.transpose(x, (1, 0, 2)).reshape(seq * bsz, d)
    for l in range(depth):
        wts = dict(
            w_in=w_in_b[l], b_in=row(b_in)[l], sc_w=sc_w[l], wp=wp[l], pscale=row(pool_scale)[l],
            cfw=cf_dw_w[l], cfb=row(cf_dw_b)[l], cfg=row(cf_ln_g)[l], cfbeta=row(cf_ln_b)[l],
            lre=lre[l], lim=lim[l], bcat=bcat[l], ccat=ccat[l], dskip=row(ssm_d)[l], wglu=w_glu_b[l],
            bglu=row(ssm_b_glu)[l], wo=w_o_b[l], ln1_g=row(ln1_g)[l], ln1_b=row(ln1_b)[l])
        xt = _mixer_layer(xt, mod[l], wts, alpha=alpha)
        xt = _ffn_layer(xt, mod[l], wg_b[l], wu_b[l], wd_b[l], row(ln2_g)[l], row(ln2_b)[l], alpha=alpha)
    return jnp.transpose(xt.reshape(seq, bsz, d), (1, 0, 2))
```

```python
import functools

import jax
import jax.numpy as jnp
from jax import lax
from jax.experimental import pallas as pl
from jax.experimental.pallas import tpu as pltpu

F32 = jnp.float32
BF16 = jnp.bfloat16

N_MIXERS = 4
SC_TAPS = 3
POOL_WINDOWS = (2, 4, 8, 16)
CF_TAPS = 31
LN_EPS = 1e-5

SUBLANES = 8
VMEM_LIMIT_BYTES = 60 * 1024 * 1024

TILE_STEPS = 32
HIST_STEPS = 32
FFN_CHUNK = 256
MOD_TN = 1536
CONV_BLOCKS = 4
SCAN_GROUPS = 4

LAYER_FLAGS = {}


def _sigmoid(v):
    return 1.0 / (1.0 + jnp.exp(-v))


def _layer_norm(v, g, b):
    mu = jnp.mean(v, axis=-1, keepdims=True)
    c = v - mu
    var = jnp.mean(c * c, axis=-1, keepdims=True)
    return c * lax.rsqrt(var + LN_EPS) * g + b


def _per_batch(v, m):
    rows, d = v.shape
    return jnp.broadcast_to(m[None], (rows // SUBLANES, SUBLANES, d)).reshape(rows, d)


def _mod_kernel(c_ref, w_ref, b_ref, o_ref):
    c = c_ref[...]
    cond = c * _sigmoid(c)
    o_ref[0] = jnp.dot(cond.astype(BF16), w_ref[0].astype(BF16), preferred_element_type=F32) + b_ref[0]


def _modulation(c, w_ada, b_ada):
    depth, d, n = w_ada.shape
    bsz = c.shape[0]
    return pl.pallas_call(
        _mod_kernel,
        grid=(depth, n // MOD_TN),
        in_specs=[
            pl.BlockSpec((bsz, d), lambda l, j: (0, 0)),
            pl.BlockSpec((1, d, MOD_TN), lambda l, j: (l, 0, j)),
            pl.BlockSpec((1, 1, MOD_TN), lambda l, j: (l, 0, j)),
        ],
        out_specs=pl.BlockSpec((1, bsz, MOD_TN), lambda l, j: (l, 0, j)),
        out_shape=jax.ShapeDtypeStruct((depth, bsz, n), F32),
        compiler_params=pltpu.CompilerParams(
            dimension_semantics=("arbitrary", "arbitrary"), vmem_limit_bytes=VMEM_LIMIT_BYTES),
        name="adaln_mod",
    )(c, w_ada, b_ada.reshape(depth, 1, n))


N_MIX_WEIGHTS = 19


def _layer_kernel(x_ref, mod_ref, *refs, alpha):
    (w_in_ref, b_in_ref, sc_w_ref, wp_ref, pscale_ref, cfw_ref, cfb_ref, cfg_ref, cfbeta_ref, lre_ref,
     lim_ref, bcat_ref, ccat_ref, dskip_ref, wglu_ref, bglu_ref, wo_ref, ln1_g_ref,
     ln1_b_ref) = refs[:N_MIX_WEIGHTS]
    wg_ref, wu_ref, wd_ref, ln2_g_ref, ln2_b_ref, o_ref = refs[N_MIX_WEIGHTS:N_MIX_WEIGHTS + 6]
    hist_ref, carry_ref, ycat_ref, hid_ref, xm_ref, z_ref, bu_ref, xs_ref, hb2_ref = refs[N_MIX_WEIGHTS + 6:]
    i = pl.program_id(0)
    rows, d = x_ref.shape
    steps = rows // SUBLANES
    gw = d // N_MIXERS
    hrows = HIST_STEPS * SUBLANES
    n_state = lre_ref.shape[1]
    d_ff = wg_ref.shape[1]
    slot = i % 2

    @pl.when(i == 0)
    def _():
        hist_ref[...] = jnp.zeros(hist_ref.shape, F32)
        carry_ref[...] = jnp.zeros(carry_ref.shape, F32)
        xm_ref[1] = jnp.zeros(xm_ref.shape[1:], F32)
        hb2_ref[...] = jnp.zeros(hb2_ref.shape, BF16)

    def ffn_chunk(k):
        def run():
            c0 = k * FFN_CHUNK
            hb2 = hb2_ref[...]
            g = jnp.dot(hb2, wg_ref[:, c0:c0 + FFN_CHUNK], preferred_element_type=F32)
            v = jnp.dot(hb2, wu_ref[:, c0:c0 + FFN_CHUNK], preferred_element_type=F32)
            hid_ref[:, c0:c0 + FFN_CHUNK] = (g * _sigmoid(g) * v).astype(BF16)
        return run

    st = {}

    def ffn_down():
        st["f"] = jnp.dot(hid_ref[...], wd_ref[...], preferred_element_type=F32)

    def ffn_norm():
        g2 = mod_ref[:, 5 * d:6 * d]
        x_prev = xm_ref[1 - slot]
        o_ref[...] = _layer_norm(alpha * x_prev + _per_batch(x_prev, 1.0 + g2) * st["f"],
                                 ln2_g_ref[...], ln2_b_ref[...])

    def mix_front():
        x = x_ref[...]
        sh1, sc1 = mod_ref[:, 0:d], mod_ref[:, d:2 * d]
        hb = (x * _per_batch(x, 1.0 + sc1) + _per_batch(x, sh1)).astype(BF16)
        z_ref[...] = jnp.dot(hb, w_in_ref[...], preferred_element_type=F32) + b_in_ref[...]
        hist_ref[0:hrows, :] = hist_ref[rows:rows + hrows, :]

    def ssm_in():
        bu_ref[...] = jnp.dot(z_ref[:, 6 * gw:7 * gw].astype(BF16), bcat_ref[...],
                              preferred_element_type=F32)

    def short_conv():
        z_h, z_b, z_c = z_ref[:, 0:gw], z_ref[:, gw:2 * gw], z_ref[:, 2 * gw:3 * gw]
        p = z_c * z_h
        hist_ref[hrows:hrows + rows, 0:gw] = p
        conv = sc_w_ref[SC_TAPS - 1:SC_TAPS, :] * p
        for k in range(SC_TAPS - 1):
            off = hrows - SUBLANES * (SC_TAPS - 1 - k)
            conv = conv + sc_w_ref[k:k + 1, :] * hist_ref[off:off + rows, 0:gw]
        ycat_ref[:, 0:gw] = (z_b * conv).astype(BF16)

    def pool():
        z_p = z_ref[:, 3 * gw:4 * gw]
        hist_ref[hrows:hrows + rows, gw:2 * gw] = z_p
        back = SUBLANES * (max(POOL_WINDOWS) - 1)
        run = hist_ref[hrows - back:hrows + rows, gw:2 * gw]
        lane = lax.broadcasted_iota(jnp.int32, (rows, gw), 1)
        pool_group = gw // len(POOL_WINDOWS)
        t_glob = i * steps + lax.broadcasted_iota(jnp.int32, (rows, gw), 0) // SUBLANES
        width = 1
        s_sel = None
        w_lane = None
        for gi, w in enumerate(POOL_WINDOWS):
            while width < w:
                sh = SUBLANES * width
                run = run[sh:] + run[:-sh]
                width *= 2
            cur = run[run.shape[0] - rows:]
            if s_sel is None:
                s_sel, w_lane = cur, jnp.full((rows, gw), float(w), F32)
            else:
                in_group = lane >= gi * pool_group
                s_sel = jnp.where(in_group, cur, s_sel)
                w_lane = jnp.where(in_group, float(w), w_lane)
        cnt = jnp.minimum((t_glob + 1).astype(F32), w_lane)
        ycat_ref[:, gw:2 * gw] = (s_sel / cnt - z_p).astype(BF16)

    def pool_mix():
        y_b = jnp.dot(ycat_ref[:, gw:2 * gw], wp_ref[...], preferred_element_type=F32) * pscale_ref[...]
        ycat_ref[:, gw:2 * gw] = y_b.astype(BF16)

    def cf_gate():
        hist_ref[hrows:hrows + rows, 2 * gw:3 * gw] = (
            z_ref[:, 4 * gw:5 * gw] * _sigmoid(z_ref[:, 5 * gw:6 * gw]))

    crow = rows // CONV_BLOCKS

    def cf_conv(blk):
        def run():
            r0 = blk * crow
            acc = jnp.broadcast_to(cfb_ref[...], (crow, gw))
            for k in range(CF_TAPS):
                off = hrows - SUBLANES * (CF_TAPS - 1 - k) + r0
                acc = acc + cfw_ref[k:k + 1, :] * hist_ref[off:off + crow, 2 * gw:3 * gw]
            hn = _layer_norm(acc, cfg_ref[...], cfbeta_ref[...])
            ycat_ref[r0:r0 + crow, 2 * gw:3 * gw] = (hn * _sigmoid(hn)).astype(BF16)
        return run

    gsteps = steps // SCAN_GROUPS

    def scan(grp):
        def run():
            a_re, a_im = lre_ref[...], lim_ref[...]
            if grp == 0:
                re, im = carry_ref[:, 0:n_state], carry_ref[:, n_state:2 * n_state]
            else:
                re, im = st["re"], st["im"]
            for s in range(grp * gsteps, (grp + 1) * gsteps, 2):
                res, ims = [], []
                for q in range(2):
                    r0 = (s + q) * SUBLANES
                    b_re = bu_ref[r0:r0 + SUBLANES, 0:n_state]
                    b_im = bu_ref[r0:r0 + SUBLANES, n_state:2 * n_state]
                    re, im = a_re * re - a_im * im + b_re, a_re * im + a_im * re + b_im
                    res.append(re)
                    ims.append(im)
                r0 = s * SUBLANES
                xs_ref[r0:r0 + 2 * SUBLANES, 0:n_state] = jnp.concatenate(res, axis=0).astype(BF16)
                xs_ref[r0:r0 + 2 * SUBLANES, n_state:2 * n_state] = jnp.concatenate(ims, axis=0).astype(BF16)
            st["re"], st["im"] = re, im
            if grp == SCAN_GROUPS - 1:
                carry_ref[:, 0:n_state] = re
                carry_ref[:, n_state:2 * n_state] = im
        return run

    def ssm_out():
        u = z_ref[:, 6 * gw:7 * gw]
        st["y_s"] = jnp.dot(xs_ref[...], ccat_ref[...], preferred_element_type=F32) + dskip_ref[...] * u

    def ssm_glu():
        y_g = jax.nn.gelu(st["y_s"])
        gate = jnp.dot(y_g.astype(BF16), wglu_ref[...], preferred_element_type=F32) + bglu_ref[...]
        ycat_ref[:, 3 * gw:4 * gw] = (y_g * _sigmoid(gate)).astype(BF16)

    def mix_out():
        st["y"] = jnp.dot(ycat_ref[...], wo_ref[...], preferred_element_type=F32)

    def mix_norm():
        x = x_ref[...]
        g1 = mod_ref[:, 2 * d:3 * d]
        x_mid = _layer_norm(alpha * x + _per_batch(x, 1.0 + g1) * st["y"], ln1_g_ref[...], ln1_b_ref[...])
        xm_ref[slot] = x_mid
        sh2, sc2 = mod_ref[:, 3 * d:4 * d], mod_ref[:, 4 * d:5 * d]
        hb2_ref[...] = (x_mid * _per_batch(x_mid, 1.0 + sc2) + _per_batch(x_mid, sh2)).astype(BF16)

    def both(*fs):
        def run():
            for f in fs:
                f()
        return run

    mixer_pieces = ([mix_front, both(short_conv, pool), both(cf_gate, ssm_in)]
                    + [cf_conv(b) for b in range(CONV_BLOCKS)] + [scan(g) for g in range(SCAN_GROUPS)])
    n_chunks = d_ff // FFN_CHUNK
    assert len(mixer_pieces) == n_chunks, (len(mixer_pieces), n_chunks)
    order = []
    for k in range(n_chunks):
        order += [ffn_chunk(k), mixer_pieces[k]]
    order += [pool_mix, ssm_out, ffn_down, ssm_glu, mix_out, ffn_norm, mix_norm]
    for piece in order:
        piece()


def _const_spec(shape):
    nd = len(shape)
    return pl.BlockSpec(shape, lambda i: (0,) * nd, pipeline_mode=pl.Buffered(1))


def _layer(xt, mod_l, mix_weights, ffn_weights, *, alpha):
    n_rows, d = xt.shape
    rows = TILE_STEPS * SUBLANES
    hrows = HIST_STEPS * SUBLANES
    gw = d // N_MIXERS
    n_tiles = n_rows // rows
    n_state2 = mix_weights[11].shape[1]
    d_ff = ffn_weights[0].shape[1]
    weights = list(mix_weights) + list(ffn_weights)
    assert len(mix_weights) == N_MIX_WEIGHTS
    return pl.pallas_call(
        functools.partial(_layer_kernel, alpha=alpha),
        grid=(n_tiles + 1,),
        in_specs=[pl.BlockSpec((rows, d), lambda i: (jnp.minimum(i, n_tiles - 1), 0)),
                  _const_spec(mod_l.shape)] + [_const_spec(w.shape) for w in weights],
        out_specs=pl.BlockSpec((rows, d), lambda i: (jnp.maximum(i - 1, 0), 0)),
        out_shape=jax.ShapeDtypeStruct((n_rows, d), F32),
        scratch_shapes=[
            pltpu.VMEM((hrows + rows, 3 * gw), F32),
            pltpu.VMEM((SUBLANES, n_state2), F32),
            pltpu.VMEM((rows, d), BF16),
            pltpu.VMEM((rows, d_ff), BF16),
            pltpu.VMEM((2, rows, d), F32),
            pltpu.VMEM((rows, 7 * gw), F32),
            pltpu.VMEM((rows, n_state2), F32),
            pltpu.VMEM((rows, n_state2), BF16),
            pltpu.VMEM((rows, d), BF16),
        ],
        compiler_params=pltpu.CompilerParams(
            dimension_semantics=("arbitrary",), vmem_limit_bytes=VMEM_LIMIT_BYTES, flags=LAYER_FLAGS),
        name="decoder_layer",
    )(xt, mod_l, *weights)


def _block_diag(blocks):
    depth, n, a, b = blocks.shape
    eye = jnp.eye(n, dtype=blocks.dtype)
    return jnp.einsum("lnab,nm->lnamb", blocks, eye).reshape(depth, n * a, n * b)


def _ssm_tables(lam_re, lam_im, log_dt, b_re, b_im, c_re, c_im):
    depth, groups, n_p = lam_re.shape
    dt = jnp.exp(log_dt)[..., None]
    mag = jnp.exp(lam_re * dt)
    bar_re = mag * jnp.cos(lam_im * dt)
    bar_im = mag * jnp.sin(lam_im * dt)
    nr, ni = bar_re - 1.0, bar_im
    den = lam_re * lam_re + lam_im * lam_im
    q_re = (nr * lam_re + ni * lam_im) / den
    q_im = (ni * lam_re - nr * lam_im) / den
    bb_re = q_re[..., None] * b_re - q_im[..., None] * b_im
    bb_im = q_re[..., None] * b_im + q_im[..., None] * b_re
    bcat = jnp.concatenate([_block_diag(jnp.swapaxes(bb_re, 2, 3)), _block_diag(jnp.swapaxes(bb_im, 2, 3))],
                           axis=2)
    ccat = jnp.concatenate([_block_diag(jnp.swapaxes(c_re, 2, 3)), -_block_diag(jnp.swapaxes(c_im, 2, 3))],
                           axis=1)
    flat = (depth, 1, groups * n_p)
    lre = jnp.broadcast_to(bar_re.reshape(flat), (depth, SUBLANES, groups * n_p))
    lim = jnp.broadcast_to(bar_im.reshape(flat), (depth, SUBLANES, groups * n_p))
    return lre, lim, bcat.astype(BF16), ccat.astype(BF16)


def kernel(x, c, w_ada, b_ada, w_in, b_in, sc_w, pool_w, pool_scale, cf_dw_w, cf_dw_b, cf_ln_g, cf_ln_b,
           ssm_lam_re, ssm_lam_im, ssm_log_dt, ssm_b_re, ssm_b_im, ssm_c_re, ssm_c_im, ssm_d,
           ssm_w_glu, ssm_b_glu, w_o, ln1_g, ln1_b, w_gate, w_up, w_down, ln2_g, ln2_b):
    bsz, seq, d = x.shape
    depth = w_in.shape[0]
    assert bsz == SUBLANES, "layout puts the batch on the 8 sublanes"
    assert seq % TILE_STEPS == 0 and HIST_STEPS >= CF_TAPS - 1 and HIST_STEPS <= TILE_STEPS
    alpha = (2 * depth) ** 0.25

    mod = _modulation(c, w_ada, b_ada)
    lre, lim, bcat, ccat = _ssm_tables(ssm_lam_re, ssm_lam_im, ssm_log_dt, ssm_b_re, ssm_b_im,
                                       ssm_c_re, ssm_c_im)
    wp = _block_diag(pool_w).astype(BF16)
    row = lambda a: a.reshape(depth, 1, a.shape[-1])
    w_in_b, w_o_b, w_glu_b = w_in.astype(BF16), w_o.astype(BF16), ssm_w_glu.astype(BF16)
    wg_b, wu_b, wd_b = w_gate.astype(BF16), w_up.astype(BF16), w_down.astype(BF16)

    xt = jnp.transpose(x, (1, 0, 2)).reshape(seq * bsz, d)
    for l in range(depth):
        mix_weights = (w_in_b[l], row(b_in)[l], sc_w[l], wp[l], row(pool_scale)[l], cf_dw_w[l],
                       row(cf_dw_b)[l], row(cf_ln_g)[l], row(cf_ln_b)[l], lre[l], lim[l], bcat[l], ccat[l],
                       row(ssm_d)[l], w_glu_b[l], row(ssm_b_glu)[l], w_o_b[l], row(ln1_g)[l], row(ln1_b)[l])
        ffn_weights = (wg_b[l], wu_b[l], wd_b[l], row(ln2_g)[l], row(ln2_b)[l])
        xt = _layer(xt, mod[l], mix_weights, ffn_weights, alpha=alpha)
    return jnp.transpose(xt.reshape(seq, bsz, d), (1, 0, 2))
```

```python
import functools

import jax
import jax.numpy as jnp
from jax import lax
from jax.experimental import pallas as pl
from jax.experimental.pallas import tpu as pltpu

F32 = jnp.float32
BF16 = jnp.bfloat16

N_MIXERS = 4
SC_TAPS = 3
POOL_WINDOWS = (2, 4, 8, 16)
CF_TAPS = 31
LN_EPS = 1e-5

SUBLANES = 8
VMEM_LIMIT_BYTES = 60 * 1024 * 1024

TILE_STEPS = 64
HIST_STEPS = 32
FFN_CHUNK = 256
MOD_TN = 1536
CONV_BLOCKS = 4
SCAN_GROUPS = 4

LAYER_FLAGS = {}


def _sigmoid(v):
    return 1.0 / (1.0 + jnp.exp(-v))


def _layer_norm(v, g, b):
    mu = jnp.mean(v, axis=-1, keepdims=True)
    c = v - mu
    var = jnp.mean(c * c, axis=-1, keepdims=True)
    return c * lax.rsqrt(var + LN_EPS) * g + b


def _per_batch(v, m):
    rows, d = v.shape
    return jnp.broadcast_to(m[None], (rows // SUBLANES, SUBLANES, d)).reshape(rows, d)


def _mod_kernel(c_ref, w_ref, b_ref, o_ref):
    c = c_ref[...]
    cond = c * _sigmoid(c)
    o_ref[0] = jnp.dot(cond.astype(BF16), w_ref[0].astype(BF16), preferred_element_type=F32) + b_ref[0]


def _modulation(c, w_ada, b_ada):
    depth, d, n = w_ada.shape
    bsz = c.shape[0]
    return pl.pallas_call(
        _mod_kernel,
        grid=(depth, n // MOD_TN),
        in_specs=[
            pl.BlockSpec((bsz, d), lambda l, j: (0, 0)),
            pl.BlockSpec((1, d, MOD_TN), lambda l, j: (l, 0, j)),
            pl.BlockSpec((1, 1, MOD_TN), lambda l, j: (l, 0, j)),
        ],
        out_specs=pl.BlockSpec((1, bsz, MOD_TN), lambda l, j: (l, 0, j)),
        out_shape=jax.ShapeDtypeStruct((depth, bsz, n), F32),
        compiler_params=pltpu.CompilerParams(
            dimension_semantics=("arbitrary", "arbitrary"), vmem_limit_bytes=VMEM_LIMIT_BYTES),
        name="adaln_mod",
    )(c, w_ada, b_ada.reshape(depth, 1, n))


N_MIX_WEIGHTS = 19


def _layer_kernel(x_ref, mod_ref, *refs, alpha):
    (w_in_ref, b_in_ref, sc_w_ref, wp_ref, pscale_ref, cfw_ref, cfb_ref, cfg_ref, cfbeta_ref, lre_ref,
     lim_ref, bcat_ref, ccat_ref, dskip_ref, wglu_ref, bglu_ref, wo_ref, ln1_g_ref,
     ln1_b_ref) = refs[:N_MIX_WEIGHTS]
    wg_ref, wu_ref, wd_ref, ln2_g_ref, ln2_b_ref, o_ref = refs[N_MIX_WEIGHTS:N_MIX_WEIGHTS + 6]
    (hist_ref, carry_ref, ycat_ref, hid_ref, xm_ref, z_ref, bu_ref, xs_ref, hb2_ref,
     ypre_ref) = refs[N_MIX_WEIGHTS + 6:]
    i = pl.program_id(0)
    rows, d = x_ref.shape
    steps = rows // SUBLANES
    gw = d // N_MIXERS
    hrows = HIST_STEPS * SUBLANES
    n_state = lre_ref.shape[1]
    d_ff = wg_ref.shape[1]

    @pl.when(i == 0)
    def _():
        hist_ref[...] = jnp.zeros(hist_ref.shape, F32)
        carry_ref[...] = jnp.zeros(carry_ref.shape, F32)
        ypre_ref[...] = jnp.zeros(ypre_ref.shape, F32)

    def ffn_chunk(k):
        def run():
            c0 = k * FFN_CHUNK
            hb2 = hb2_ref[...]
            g = jnp.dot(hb2, wg_ref[:, c0:c0 + FFN_CHUNK], preferred_element_type=F32)
            v = jnp.dot(hb2, wu_ref[:, c0:c0 + FFN_CHUNK], preferred_element_type=F32)
            hid_ref[:, c0:c0 + FFN_CHUNK] = (g * _sigmoid(g) * v).astype(BF16)
        return run

    st = {}

    def ffn_down():
        st["f"] = jnp.dot(hid_ref[...], wd_ref[...], preferred_element_type=F32)

    def ffn_norm():
        g2 = mod_ref[:, 5 * d:6 * d]
        x_prev = xm_ref[...]
        o_ref[...] = _layer_norm(alpha * x_prev + _per_batch(x_prev, 1.0 + g2) * st["f"],
                                 ln2_g_ref[...], ln2_b_ref[...])

    def mix_front():
        x = x_ref[...]
        sh1, sc1 = mod_ref[:, 0:d], mod_ref[:, d:2 * d]
        hb = (x * _per_batch(x, 1.0 + sc1) + _per_batch(x, sh1)).astype(BF16)
        z_ref[...] = jnp.dot(hb, w_in_ref[...], preferred_element_type=F32) + b_in_ref[...]
        hist_ref[0:hrows, :] = hist_ref[rows:rows + hrows, :]

    def ssm_in():
        bu_ref[...] = jnp.dot(z_ref[:, 6 * gw:7 * gw].astype(BF16), bcat_ref[...],
                              preferred_element_type=F32)

    def short_conv():
        z_h, z_b, z_c = z_ref[:, 0:gw], z_ref[:, gw:2 * gw], z_ref[:, 2 * gw:3 * gw]
        p = z_c * z_h
        hist_ref[hrows:hrows + rows, 0:gw] = p
        conv = sc_w_ref[SC_TAPS - 1:SC_TAPS, :] * p
        for k in range(SC_TAPS - 1):
            off = hrows - SUBLANES * (SC_TAPS - 1 - k)
            conv = conv + sc_w_ref[k:k + 1, :] * hist_ref[off:off + rows, 0:gw]
        ycat_ref[:, 0:gw] = (z_b * conv).astype(BF16)

    def pool():
        z_p = z_ref[:, 3 * gw:4 * gw]
        hist_ref[hrows:hrows + rows, gw:2 * gw] = z_p
        back = SUBLANES * (max(POOL_WINDOWS) - 1)
        run = hist_ref[hrows - back:hrows + rows, gw:2 * gw]
        lane = lax.broadcasted_iota(jnp.int32, (rows, gw), 1)
        pool_group = gw // len(POOL_WINDOWS)
        t_glob = i * steps + lax.broadcasted_iota(jnp.int32, (rows, gw), 0) // SUBLANES
        width = 1
        s_sel = None
        w_lane = None
        for gi, w in enumerate(POOL_WINDOWS):
            while width < w:
                sh = SUBLANES * width
                run = run[sh:] + run[:-sh]
                width *= 2
            cur = run[run.shape[0] - rows:]
            if s_sel is None:
                s_sel, w_lane = cur, jnp.full((rows, gw), float(w), F32)
            else:
                in_group = lane >= gi * pool_group
                s_sel = jnp.where(in_group, cur, s_sel)
                w_lane = jnp.where(in_group, float(w), w_lane)
        cnt = jnp.minimum((t_glob + 1).astype(F32), w_lane)
        ycat_ref[:, gw:2 * gw] = (s_sel / cnt - z_p).astype(BF16)

    def pool_mix():
        y_b = jnp.dot(ycat_ref[:, gw:2 * gw], wp_ref[...], preferred_element_type=F32) * pscale_ref[...]
        ycat_ref[:, gw:2 * gw] = y_b.astype(BF16)

    def cf_gate():
        hist_ref[hrows:hrows + rows, 2 * gw:3 * gw] = (
            z_ref[:, 4 * gw:5 * gw] * _sigmoid(z_ref[:, 5 * gw:6 * gw]))

    crow = rows // CONV_BLOCKS

    def cf_conv(blk):
        def run():
            r0 = blk * crow
            acc = jnp.broadcast_to(cfb_ref[...], (crow, gw))
            for k in range(CF_TAPS):
                off = hrows - SUBLANES * (CF_TAPS - 1 - k) + r0
                acc = acc + cfw_ref[k:k + 1, :] * hist_ref[off:off + crow, 2 * gw:3 * gw]
            hn = _layer_norm(acc, cfg_ref[...], cfbeta_ref[...])
            ycat_ref[r0:r0 + crow, 2 * gw:3 * gw] = (hn * _sigmoid(hn)).astype(BF16)
        return run

    gsteps = steps // SCAN_GROUPS

    def scan(grp):
        def run():
            a_re, a_im = lre_ref[...], lim_ref[...]
            if grp == 0:
                re, im = carry_ref[:, 0:n_state], carry_ref[:, n_state:2 * n_state]
            else:
                re, im = st["re"], st["im"]
            for s in range(grp * gsteps, (grp + 1) * gsteps, 2):
                res, ims = [], []
                for q in range(2):
                    r0 = (s + q) * SUBLANES
                    b_re = bu_ref[r0:r0 + SUBLANES, 0:n_state]
                    b_im = bu_ref[r0:r0 + SUBLANES, n_state:2 * n_state]
                    re, im = a_re * re - a_im * im + b_re, a_re * im + a_im * re + b_im
                    res.append(re)
                    ims.append(im)
                r0 = s * SUBLANES
                xs_ref[r0:r0 + 2 * SUBLANES, 0:n_state] = jnp.concatenate(res, axis=0).astype(BF16)
                xs_ref[r0:r0 + 2 * SUBLANES, n_state:2 * n_state] = jnp.concatenate(ims, axis=0).astype(BF16)
            st["re"], st["im"] = re, im
            if grp == SCAN_GROUPS - 1:
                carry_ref[:, 0:n_state] = re
                carry_ref[:, n_state:2 * n_state] = im
        return run

    def ssm_out():
        u = z_ref[:, 6 * gw:7 * gw]
        st["y_s"] = jnp.dot(xs_ref[...], ccat_ref[...], preferred_element_type=F32) + dskip_ref[...] * u

    def ssm_glu():
        y_g = jax.nn.gelu(st["y_s"])
        gate = jnp.dot(y_g.astype(BF16), wglu_ref[...], preferred_element_type=F32) + bglu_ref[...]
        ycat_ref[:, 3 * gw:4 * gw] = (y_g * _sigmoid(gate)).astype(BF16)

    def mix_out():
        st["y"] = jnp.dot(ycat_ref[...], wo_ref[...], preferred_element_type=F32)

    def mix_residual():
        x = x_ref[...]
        g1 = mod_ref[:, 2 * d:3 * d]
        ypre_ref[...] = alpha * x + _per_batch(x, 1.0 + g1) * st["y"]

    def prev_norm():
        x_mid = _layer_norm(ypre_ref[...], ln1_g_ref[...], ln1_b_ref[...])
        xm_ref[...] = x_mid
        sh2, sc2 = mod_ref[:, 3 * d:4 * d], mod_ref[:, 4 * d:5 * d]
        hb2_ref[...] = (x_mid * _per_batch(x_mid, 1.0 + sc2) + _per_batch(x_mid, sh2)).astype(BF16)

    def both(*fs):
        def run():
            for f in fs:
                f()
        return run

    mixer_pieces = ([both(mix_front, prev_norm), both(short_conv, pool), both(cf_gate, ssm_in)]
                    + [cf_conv(b) for b in range(CONV_BLOCKS)] + [scan(g) for g in range(SCAN_GROUPS)])
    n_chunks = d_ff // FFN_CHUNK
    assert len(mixer_pieces) == n_chunks, (len(mixer_pieces), n_chunks)
    order = [mixer_pieces[0]]
    for k in range(n_chunks):
        order += [ffn_chunk(k)] + ([mixer_pieces[k + 1]] if k + 1 < n_chunks else [])
    order += [pool_mix, ssm_out, ffn_down, ssm_glu, mix_out, ffn_norm, mix_residual]
    for piece in order:
        piece()


def _const_spec(shape):
    nd = len(shape)
    return pl.BlockSpec(shape, lambda i: (0,) * nd, pipeline_mode=pl.Buffered(1))


def _layer(xt, mod_l, mix_weights, ffn_weights, *, alpha):
    n_rows, d = xt.shape
    rows = TILE_STEPS * SUBLANES
    hrows = HIST_STEPS * SUBLANES
    gw = d // N_MIXERS
    n_tiles = n_rows // rows
    n_state2 = mix_weights[11].shape[1]
    d_ff = ffn_weights[0].shape[1]
    weights = list(mix_weights) + list(ffn_weights)
    assert len(mix_weights) == N_MIX_WEIGHTS
    return pl.pallas_call(
        functools.partial(_layer_kernel, alpha=alpha),
        grid=(n_tiles + 1,),
        in_specs=[pl.BlockSpec((rows, d), lambda i: (jnp.minimum(i, n_tiles - 1), 0)),
                  _const_spec(mod_l.shape)] + [_const_spec(w.shape) for w in weights],
        out_specs=pl.BlockSpec((rows, d), lambda i: (jnp.maximum(i - 1, 0), 0)),
        out_shape=jax.ShapeDtypeStruct((n_rows, d), F32),
        scratch_shapes=[
            pltpu.VMEM((hrows + rows, 3 * gw), F32),
            pltpu.VMEM((SUBLANES, n_state2), F32),
            pltpu.VMEM((rows, d), BF16),
            pltpu.VMEM((rows, d_ff), BF16),
            pltpu.VMEM((rows, d), F32),
            pltpu.VMEM((rows, 7 * gw), F32),
            pltpu.VMEM((rows, n_state2), F32),
            pltpu.VMEM((rows, n_state2), BF16),
            pltpu.VMEM((rows, d), BF16),
            pltpu.VMEM((rows, d), F32),
        ],
        compiler_params=pltpu.CompilerParams(
            dimension_semantics=("arbitrary",), vmem_limit_bytes=VMEM_LIMIT_BYTES, flags=LAYER_FLAGS),
        name="decoder_layer",
    )(xt, mod_l, *weights)


def _block_diag(blocks):
    depth, n, a, b = blocks.shape
    eye = jnp.eye(n, dtype=blocks.dtype)
    return jnp.einsum("lnab,nm->lnamb", blocks, eye).reshape(depth, n * a, n * b)


def _ssm_tables(lam_re, lam_im, log_dt, b_re, b_im, c_re, c_im):
    depth, groups, n_p = lam_re.shape
    dt = jnp.exp(log_dt)[..., None]
    mag = jnp.exp(lam_re * dt)
    bar_re = mag * jnp.cos(lam_im * dt)
    bar_im = mag * jnp.sin(lam_im * dt)
    nr, ni = bar_re - 1.0, bar_im
    den = lam_re * lam_re + lam_im * lam_im
    q_re = (nr * lam_re + ni * lam_im) / den
    q_im = (ni * lam_re - nr * lam_im) / den
    bb_re = q_re[..., None] * b_re - q_im[..., None] * b_im
    bb_im = q_re[..., None] * b_im + q_im[..., None] * b_re
    bcat = jnp.concatenate([_block_diag(jnp.swapaxes(bb_re, 2, 3)), _block_diag(jnp.swapaxes(bb_im, 2, 3))],
                           axis=2)
    ccat = jnp.concatenate([_block_diag(jnp.swapaxes(c_re, 2, 3)), -_block_diag(jnp.swapaxes(c_im, 2, 3))],
                           axis=1)
    flat = (depth, 1, groups * n_p)
    lre = jnp.broadcast_to(bar_re.reshape(flat), (depth, SUBLANES, groups * n_p))
    lim = jnp.broadcast_to(bar_im.reshape(flat), (depth, SUBLANES, groups * n_p))
    return lre, lim, bcat.astype(BF16), ccat.astype(BF16)


def kernel(x, c, w_ada, b_ada, w_in, b_in, sc_w, pool_w, pool_scale, cf_dw_w, cf_dw_b, cf_ln_g, cf_ln_b,
           ssm_lam_re, ssm_lam_im, ssm_log_dt, ssm_b_re, ssm_b_im, ssm_c_re, ssm_c_im, ssm_d,
           ssm_w_glu, ssm_b_glu, w_o, ln1_g, ln1_b, w_gate, w_up, w_down, ln2_g, ln2_b):
    bsz, seq, d = x.shape
    depth = w_in.shape[0]
    assert bsz == SUBLANES, "layout puts the batch on the 8 sublanes"
    assert seq % TILE_STEPS == 0 and HIST_STEPS >= CF_TAPS - 1 and HIST_STEPS <= TILE_STEPS
    alpha = (2 * depth) ** 0.25

    mod = _modulation(c, w_ada, b_ada)
    lre, lim, bcat, ccat = _ssm_tables(ssm_lam_re, ssm_lam_im, ssm_log_dt, ssm_b_re, ssm_b_im,
                                       ssm_c_re, ssm_c_im)
    wp = _block_diag(pool_w).astype(BF16)
    row = lambda a: a.reshape(depth, 1, a.shape[-1])
    w_in_b, w_o_b, w_glu_b = w_in.astype(BF16), w_o.astype(BF16), ssm_w_glu.astype(BF16)
    wg_b, wu_b, wd_b = w_gate.astype(BF16), w_up.astype(BF16), w_down.astype(BF16)

    xt = jnp.transpose(x, (1, 0, 2)).reshape(seq * bsz, d)
    for l in range(depth):
        mix_weights = (w_in_b[l], row(b_in)[l], sc_w[l], wp[l], row(pool_scale)[l], cf_dw_w[l],
                       row(cf_dw_b)[l], row(cf_ln_g)[l], row(cf_ln_b)[l], lre[l], lim[l], bcat[l], ccat[l],
                       row(ssm_d)[l], w_glu_b[l], row(ssm_b_glu)[l], w_o_b[l], row(ln1_g)[l], row(ln1_b)[l])
        ffn_weights = (wg_b[l], wu_b[l], wd_b[l], row(ln2_g)[l], row(ln2_b)[l])
        xt = _layer(xt, mod[l], mix_weights, ffn_weights, alpha=alpha)
    return jnp.transpose(xt.reshape(seq, bsz, d), (1, 0, 2))
```

```python
import functools

import jax
import jax.numpy as jnp
from jax import lax
from jax.experimental import pallas as pl
from jax.experimental.pallas import tpu as pltpu

F32 = jnp.float32
BF16 = jnp.bfloat16

N_MIXERS = 4
SC_TAPS = 3
POOL_WINDOWS = (2, 4, 8, 16)
CF_TAPS = 31
LN_EPS = 1e-5

SUBLANES = 8
VMEM_LIMIT_BYTES = 60 * 1024 * 1024

TILE_STEPS = 64
HIST_STEPS = 32
FFN_CHUNK = 256
MOD_TN = 1536
CONV_BLOCKS = 4
SCAN_GROUPS = 4

LAYER_FLAGS = {}


def _sigmoid(v):
    return 1.0 / (1.0 + jnp.exp(-v))


def _layer_norm(v, g, b):
    mu = jnp.mean(v, axis=-1, keepdims=True)
    c = v - mu
    var = jnp.mean(c * c, axis=-1, keepdims=True)
    return c * lax.rsqrt(var + LN_EPS) * g + b


def _per_batch(v, m):
    rows, d = v.shape
    return jnp.broadcast_to(m[None], (rows // SUBLANES, SUBLANES, d)).reshape(rows, d)


def _mod_kernel(c_ref, w_ref, b_ref, o_ref):
    c = c_ref[...]
    cond = c * _sigmoid(c)
    o_ref[0] = jnp.dot(cond.astype(BF16), w_ref[0].astype(BF16), preferred_element_type=F32) + b_ref[0]


def _modulation(c, w_ada, b_ada):
    depth, d, n = w_ada.shape
    bsz = c.shape[0]
    return pl.pallas_call(
        _mod_kernel,
        grid=(depth, n // MOD_TN),
        in_specs=[
            pl.BlockSpec((bsz, d), lambda l, j: (0, 0)),
            pl.BlockSpec((1, d, MOD_TN), lambda l, j: (l, 0, j)),
            pl.BlockSpec((1, 1, MOD_TN), lambda l, j: (l, 0, j)),
        ],
        out_specs=pl.BlockSpec((1, bsz, MOD_TN), lambda l, j: (l, 0, j)),
        out_shape=jax.ShapeDtypeStruct((depth, bsz, n), F32),
        compiler_params=pltpu.CompilerParams(
            dimension_semantics=("arbitrary", "arbitrary"), vmem_limit_bytes=VMEM_LIMIT_BYTES),
        name="adaln_mod",
    )(c, w_ada, b_ada.reshape(depth, 1, n))


N_MIX_WEIGHTS = 19


def _layer_kernel(x_ref, mod_ref, *refs, alpha, x_batch_major, out_batch_major):
    (w_in_ref, b_in_ref, sc_w_ref, wp_ref, pscale_ref, cfw_ref, cfb_ref, cfg_ref, cfbeta_ref, lre_ref,
     lim_ref, bcat_ref, ccat_ref, dskip_ref, wglu_ref, bglu_ref, wo_ref, ln1_g_ref,
     ln1_b_ref) = refs[:N_MIX_WEIGHTS]
    wg_ref, wu_ref, wd_ref, ln2_g_ref, ln2_b_ref, o_ref = refs[N_MIX_WEIGHTS:N_MIX_WEIGHTS + 6]
    (hist_ref, carry_ref, ycat_ref, hid_ref, xm_ref, z_ref, bu_ref, xs_ref, hb2_ref,
     ypre_ref) = refs[N_MIX_WEIGHTS + 6:N_MIX_WEIGHTS + 16]
    i = pl.program_id(0)
    rows, d = xm_ref.shape
    steps = rows // SUBLANES
    gw = d // N_MIXERS
    hrows = HIST_STEPS * SUBLANES
    n_state = lre_ref.shape[1]
    d_ff = wg_ref.shape[1]

    @pl.when(i == 0)
    def _():
        hist_ref[...] = jnp.zeros(hist_ref.shape, F32)
        carry_ref[...] = jnp.zeros(carry_ref.shape, F32)
        ypre_ref[...] = jnp.zeros(ypre_ref.shape, F32)

    def ffn_chunk(k):
        def run():
            c0 = k * FFN_CHUNK
            hb2 = hb2_ref[...]
            g = jnp.dot(hb2, wg_ref[:, c0:c0 + FFN_CHUNK], preferred_element_type=F32)
            v = jnp.dot(hb2, wu_ref[:, c0:c0 + FFN_CHUNK], preferred_element_type=F32)
            hid_ref[:, c0:c0 + FFN_CHUNK] = (g * _sigmoid(g) * v).astype(BF16)
        return run

    st = {}

    def ffn_down():
        st["f"] = jnp.dot(hid_ref[...], wd_ref[...], preferred_element_type=F32)

    def ffn_norm():
        g2 = mod_ref[:, 5 * d:6 * d]
        x_prev = xm_ref[...]
        out = _layer_norm(alpha * x_prev + _per_batch(x_prev, 1.0 + g2) * st["f"],
                          ln2_g_ref[...], ln2_b_ref[...])
        if out_batch_major:
            out = jnp.swapaxes(out.reshape(steps, SUBLANES, d), 0, 1)
        o_ref[...] = out

    if x_batch_major:
        xt_ref = refs[N_MIX_WEIGHTS + 16]

        def load_x():
            xt_ref[...] = jnp.swapaxes(x_ref[...], 0, 1).reshape(rows, d)
    else:
        xt_ref = x_ref

        def load_x():
            pass

    def mix_front():
        x = xt_ref[...]
        sh1, sc1 = mod_ref[:, 0:d], mod_ref[:, d:2 * d]
        hb = (x * _per_batch(x, 1.0 + sc1) + _per_batch(x, sh1)).astype(BF16)
        z_ref[...] = jnp.dot(hb, w_in_ref[...], preferred_element_type=F32) + b_in_ref[...]
        hist_ref[0:hrows, :] = hist_ref[rows:rows + hrows, :]

    def ssm_in():
        bu_ref[...] = jnp.dot(z_ref[:, 6 * gw:7 * gw].astype(BF16), bcat_ref[...],
                              preferred_element_type=F32)

    def short_conv():
        z_h, z_b, z_c = z_ref[:, 0:gw], z_ref[:, gw:2 * gw], z_ref[:, 2 * gw:3 * gw]
        p = z_c * z_h
        hist_ref[hrows:hrows + rows, 0:gw] = p
        conv = sc_w_ref[SC_TAPS - 1:SC_TAPS, :] * p
        for k in range(SC_TAPS - 1):
            off = hrows - SUBLANES * (SC_TAPS - 1 - k)
            conv = conv + sc_w_ref[k:k + 1, :] * hist_ref[off:off + rows, 0:gw]
        ycat_ref[:, 0:gw] = (z_b * conv).astype(BF16)

    def pool():
        z_p = z_ref[:, 3 * gw:4 * gw]
        hist_ref[hrows:hrows + rows, gw:2 * gw] = z_p
        back = SUBLANES * (max(POOL_WINDOWS) - 1)
        run = hist_ref[hrows - back:hrows + rows, gw:2 * gw]
        lane = lax.broadcasted_iota(jnp.int32, (rows, gw), 1)
        pool_group = gw // len(POOL_WINDOWS)
        t_glob = i * steps + lax.broadcasted_iota(jnp.int32, (rows, gw), 0) // SUBLANES
        width = 1
        s_sel = None
        w_lane = None
        for gi, w in enumerate(POOL_WINDOWS):
            while width < w:
                sh = SUBLANES * width
                run = run[sh:] + run[:-sh]
                width *= 2
            cur = run[run.shape[0] - rows:]
            if s_sel is None:
                s_sel, w_lane = cur, jnp.full((rows, gw), float(w), F32)
            else:
                in_group = lane >= gi * pool_group
                s_sel = jnp.where(in_group, cur, s_sel)
                w_lane = jnp.where(in_group, float(w), w_lane)
        cnt = jnp.minimum((t_glob + 1).astype(F32), w_lane)
        ycat_ref[:, gw:2 * gw] = (s_sel / cnt - z_p).astype(BF16)

    def pool_mix():
        y_b = jnp.dot(ycat_ref[:, gw:2 * gw], wp_ref[...], preferred_element_type=F32) * pscale_ref[...]
        ycat_ref[:, gw:2 * gw] = y_b.astype(BF16)

    def cf_gate():
        hist_ref[hrows:hrows + rows, 2 * gw:3 * gw] = (
            z_ref[:, 4 * gw:5 * gw] * _sigmoid(z_ref[:, 5 * gw:6 * gw]))

    crow = rows // CONV_BLOCKS

    def cf_conv(blk):
        def run():
            r0 = blk * crow
            acc = jnp.broadcast_to(cfb_ref[...], (crow, gw))
            for k in range(CF_TAPS):
                off = hrows - SUBLANES * (CF_TAPS - 1 - k) + r0
                acc = acc + cfw_ref[k:k + 1, :] * hist_ref[off:off + crow, 2 * gw:3 * gw]
            hn = _layer_norm(acc, cfg_ref[...], cfbeta_ref[...])
            ycat_ref[r0:r0 + crow, 2 * gw:3 * gw] = (hn * _sigmoid(hn)).astype(BF16)
        return run

    gsteps = steps // SCAN_GROUPS

    def scan(grp):
        def run():
            a_re, a_im = lre_ref[...], lim_ref[...]
            if grp == 0:
                re, im = carry_ref[:, 0:n_state], carry_ref[:, n_state:2 * n_state]
            else:
                re, im = st["re"], st["im"]
            for s in range(grp * gsteps, (grp + 1) * gsteps, 2):
                res, ims = [], []
                for q in range(2):
                    r0 = (s + q) * SUBLANES
                    b_re = bu_ref[r0:r0 + SUBLANES, 0:n_state]
                    b_im = bu_ref[r0:r0 + SUBLANES, n_state:2 * n_state]
                    re, im = a_re * re - a_im * im + b_re, a_re * im + a_im * re + b_im
                    res.append(re)
                    ims.append(im)
                r0 = s * SUBLANES
                xs_ref[r0:r0 + 2 * SUBLANES, 0:n_state] = jnp.concatenate(res, axis=0).astype(BF16)
                xs_ref[r0:r0 + 2 * SUBLANES, n_state:2 * n_state] = jnp.concatenate(ims, axis=0).astype(BF16)
            st["re"], st["im"] = re, im
            if grp == SCAN_GROUPS - 1:
                carry_ref[:, 0:n_state] = re
                carry_ref[:, n_state:2 * n_state] = im
        return run

    def ssm_out():
        u = z_ref[:, 6 * gw:7 * gw]
        st["y_s"] = jnp.dot(xs_ref[...], ccat_ref[...], preferred_element_type=F32) + dskip_ref[...] * u

    def ssm_glu():
        y_g = jax.nn.gelu(st["y_s"])
        gate = jnp.dot(y_g.astype(BF16), wglu_ref[...], preferred_element_type=F32) + bglu_ref[...]
        ycat_ref[:, 3 * gw:4 * gw] = (y_g * _sigmoid(gate)).astype(BF16)

    def mix_out():
        st["y"] = jnp.dot(ycat_ref[...], wo_ref[...], preferred_element_type=F32)

    def mix_residual():
        x = xt_ref[...]
        g1 = mod_ref[:, 2 * d:3 * d]
        ypre_ref[...] = alpha * x + _per_batch(x, 1.0 + g1) * st["y"]

    def prev_norm():
        x_mid = _layer_norm(ypre_ref[...], ln1_g_ref[...], ln1_b_ref[...])
        xm_ref[...] = x_mid
        sh2, sc2 = mod_ref[:, 3 * d:4 * d], mod_ref[:, 4 * d:5 * d]
        hb2_ref[...] = (x_mid * _per_batch(x_mid, 1.0 + sc2) + _per_batch(x_mid, sh2)).astype(BF16)

    def both(*fs):
        def run():
            for f in fs:
                f()
        return run

    mixer_pieces = ([both(mix_front, prev_norm), both(short_conv, pool), both(cf_gate, ssm_in)]
                    + [cf_conv(b) for b in range(CONV_BLOCKS)] + [scan(g) for g in range(SCAN_GROUPS)])
    n_chunks = d_ff // FFN_CHUNK
    assert len(mixer_pieces) == n_chunks, (len(mixer_pieces), n_chunks)
    order = [load_x, mixer_pieces[0]]
    for k in range(n_chunks):
        order += [ffn_chunk(k)] + ([mixer_pieces[k + 1]] if k + 1 < n_chunks else [])
    order += [pool_mix, ssm_out, ffn_down, ssm_glu, mix_out, ffn_norm, mix_residual]
    for piece in order:
        piece()


def _const_spec(shape):
    nd = len(shape)
    return pl.BlockSpec(shape, lambda i: (0,) * nd, pipeline_mode=pl.Buffered(1))


def _layer(xt, mod_l, mix_weights, ffn_weights, *, alpha, x_batch_major, out_batch_major):
    if x_batch_major:
        bsz, seq, d = xt.shape
        n_rows = bsz * seq
    else:
        n_rows, d = xt.shape
    rows = TILE_STEPS * SUBLANES
    hrows = HIST_STEPS * SUBLANES
    gw = d // N_MIXERS
    n_tiles = n_rows // rows
    n_state2 = mix_weights[11].shape[1]
    d_ff = ffn_weights[0].shape[1]
    weights = list(mix_weights) + list(ffn_weights)
    assert len(mix_weights) == N_MIX_WEIGHTS
    if x_batch_major:
        x_spec = pl.BlockSpec((SUBLANES, TILE_STEPS, d), lambda i: (0, jnp.minimum(i, n_tiles - 1), 0))
    else:
        x_spec = pl.BlockSpec((rows, d), lambda i: (jnp.minimum(i, n_tiles - 1), 0))
    if out_batch_major:
        out_spec = pl.BlockSpec((SUBLANES, TILE_STEPS, d), lambda i: (0, jnp.maximum(i - 1, 0), 0))
        out_shape = jax.ShapeDtypeStruct((SUBLANES, n_rows // SUBLANES, d), F32)
    else:
        out_spec = pl.BlockSpec((rows, d), lambda i: (jnp.maximum(i - 1, 0), 0))
        out_shape = jax.ShapeDtypeStruct((n_rows, d), F32)
    extra_scratch = [pltpu.VMEM((rows, d), F32)] if x_batch_major else []
    return pl.pallas_call(
        functools.partial(_layer_kernel, alpha=alpha, x_batch_major=x_batch_major,
                          out_batch_major=out_batch_major),
        grid=(n_tiles + 1,),
        in_specs=[x_spec, _const_spec(mod_l.shape)] + [_const_spec(w.shape) for w in weights],
        out_specs=out_spec,
        out_shape=out_shape,
        scratch_shapes=[
            pltpu.VMEM((hrows + rows, 3 * gw), F32),
            pltpu.VMEM((SUBLANES, n_state2), F32),
            pltpu.VMEM((rows, d), BF16),
            pltpu.VMEM((rows, d_ff), BF16),
            pltpu.VMEM((rows, d), F32),
            pltpu.VMEM((rows, 7 * gw), F32),
            pltpu.VMEM((rows, n_state2), F32),
            pltpu.VMEM((rows, n_state2), BF16),
            pltpu.VMEM((rows, d), BF16),
            pltpu.VMEM((rows, d), F32),
        ] + extra_scratch,
        compiler_params=pltpu.CompilerParams(
            dimension_semantics=("arbitrary",), vmem_limit_bytes=VMEM_LIMIT_BYTES, flags=LAYER_FLAGS),
        name="decoder_layer",
    )(xt, mod_l, *weights)


def _block_diag(blocks):
    depth, n, a, b = blocks.shape
    eye = jnp.eye(n, dtype=blocks.dtype)
    return jnp.einsum("lnab,nm->lnamb", blocks, eye).reshape(depth, n * a, n * b)


def _ssm_tables(lam_re, lam_im, log_dt, b_re, b_im, c_re, c_im):
    depth, groups, n_p = lam_re.shape
    dt = jnp.exp(log_dt)[..., None]
    mag = jnp.exp(lam_re * dt)
    bar_re = mag * jnp.cos(lam_im * dt)
    bar_im = mag * jnp.sin(lam_im * dt)
    nr, ni = bar_re - 1.0, bar_im
    den = lam_re * lam_re + lam_im * lam_im
    q_re = (nr * lam_re + ni * lam_im) / den
    q_im = (ni * lam_re - nr * lam_im) / den
    bb_re = q_re[..., None] * b_re - q_im[..., None] * b_im
    bb_im = q_re[..., None] * b_im + q_im[..., None] * b_re
    bcat = jnp.concatenate([_block_diag(jnp.swapaxes(bb_re, 2, 3)), _block_diag(jnp.swapaxes(bb_im, 2, 3))],
                           axis=2)
    ccat = jnp.concatenate([_block_diag(jnp.swapaxes(c_re, 2, 3)), -_block_diag(jnp.swapaxes(c_im, 2, 3))],
                           axis=1)
    flat = (depth, 1, groups * n_p)
    lre = jnp.broadcast_to(bar_re.reshape(flat), (depth, SUBLANES, groups * n_p))
    lim = jnp.broadcast_to(bar_im.reshape(flat), (depth, SUBLANES, groups * n_p))
    return lre, lim, bcat.astype(BF16), ccat.astype(BF16)


def kernel(x, c, w_ada, b_ada, w_in, b_in, sc_w, pool_w, pool_scale, cf_dw_w, cf_dw_b, cf_ln_g, cf_ln_b,
           ssm_lam_re, ssm_lam_im, ssm_log_dt, ssm_b_re, ssm_b_im, ssm_c_re, ssm_c_im, ssm_d,
           ssm_w_glu, ssm_b_glu, w_o, ln1_g, ln1_b, w_gate, w_up, w_down, ln2_g, ln2_b):
    bsz, seq, d = x.shape
    depth = w_in.shape[0]
    assert bsz == SUBLANES, "layout puts the batch on the 8 sublanes"
    assert seq % TILE_STEPS == 0 and HIST_STEPS >= CF_TAPS - 1 and HIST_STEPS <= TILE_STEPS
    alpha = (2 * depth) ** 0.25

    mod = _modulation(c, w_ada, b_ada)
    lre, lim, bcat, ccat = _ssm_tables(ssm_lam_re, ssm_lam_im, ssm_log_dt, ssm_b_re, ssm_b_im,
                                       ssm_c_re, ssm_c_im)
    wp = _block_diag(pool_w).astype(BF16)
    row = lambda a: a.reshape(depth, 1, a.shape[-1])
    w_in_b, w_o_b, w_glu_b = w_in.astype(BF16), w_o.astype(BF16), ssm_w_glu.astype(BF16)
    wg_b, wu_b, wd_b = w_gate.astype(BF16), w_up.astype(BF16), w_down.astype(BF16)

    xt = x
    for l in range(depth):
        mix_weights = (w_in_b[l], row(b_in)[l], sc_w[l], wp[l], row(pool_scale)[l], cf_dw_w[l],
                       row(cf_dw_b)[l], row(cf_ln_g)[l], row(cf_ln_b)[l], lre[l], lim[l], bcat[l], ccat[l],
                       row(ssm_d)[l], w_glu_b[l], row(ssm_b_glu)[l], w_o_b[l], row(ln1_g)[l], row(ln1_b)[l])
        ffn_weights = (wg_b[l], wu_b[l], wd_b[l], row(ln2_g)[l], row(ln2_b)[l])
        xt = _layer(xt, mod[l], mix_weights, ffn_weights, alpha=alpha, x_batch_major=(l == 0),
                    out_batch_major=(l == depth - 1))
    return xt
```

```python
import functools

import jax
import jax.numpy as jnp
from jax import lax
from jax.experimental import pallas as pl
from jax.experimental.pallas import tpu as pltpu

F32 = jnp.float32
BF16 = jnp.bfloat16

N_MIXERS = 4
SC_TAPS = 3
POOL_WINDOWS = (2, 4, 8, 16)
CF_TAPS = 31
LN_EPS = 1e-5

SUBLANES = 8
VMEM_LIMIT_BYTES = 60 * 1024 * 1024

TILE_STEPS = 64
HIST_STEPS = 32
FFN_CHUNK = 256
MOD_TN = 1536
CONV_BLOCKS = 4
SCAN_GROUPS = 4

LAYER_FLAGS = {}


def _sigmoid(v):
    return 1.0 / (1.0 + jnp.exp(-v))


def _layer_norm(v, g, b):
    mu = jnp.mean(v, axis=-1, keepdims=True)
    c = v - mu
    var = jnp.mean(c * c, axis=-1, keepdims=True)
    return c * lax.rsqrt(var + LN_EPS) * g + b


def _per_batch(v, m):
    rows, d = v.shape
    return jnp.broadcast_to(m[None], (rows // SUBLANES, SUBLANES, d)).reshape(rows, d)


def _mod_kernel(c_ref, w_ref, b_ref, o_ref):
    c = c_ref[...]
    cond = c * _sigmoid(c)
    o_ref[0] = jnp.dot(cond.astype(BF16), w_ref[0].astype(BF16), preferred_element_type=F32) + b_ref[0]


def _modulation(c, w_ada, b_ada):
    depth, d, n = w_ada.shape
    bsz = c.shape[0]
    return pl.pallas_call(
        _mod_kernel,
        grid=(depth, n // MOD_TN),
        in_specs=[
            pl.BlockSpec((bsz, d), lambda l, j: (0, 0)),
            pl.BlockSpec((1, d, MOD_TN), lambda l, j: (l, 0, j)),
            pl.BlockSpec((1, 1, MOD_TN), lambda l, j: (l, 0, j)),
        ],
        out_specs=pl.BlockSpec((1, bsz, MOD_TN), lambda l, j: (l, 0, j)),
        out_shape=jax.ShapeDtypeStruct((depth, bsz, n), F32),
        compiler_params=pltpu.CompilerParams(
            dimension_semantics=("arbitrary", "arbitrary"), vmem_limit_bytes=VMEM_LIMIT_BYTES),
        name="adaln_mod",
    )(c, w_ada, b_ada.reshape(depth, 1, n))


N_MIX_WEIGHTS = 19


def _layer_kernel(x_ref, mod_ref, *refs, alpha, x_batch_major, out_batch_major):
    (w_in_ref, b_in_ref, sc_w_ref, wp_ref, pscale_ref, cfw_ref, cfb_ref, cfg_ref, cfbeta_ref, lre_ref,
     lim_ref, bcat_ref, ccat_ref, dskip_ref, wglu_ref, bglu_ref, wo_ref, ln1_g_ref,
     ln1_b_ref) = refs[:N_MIX_WEIGHTS]
    wg_ref, wu_ref, wd_ref, ln2_g_ref, ln2_b_ref, o_ref = refs[N_MIX_WEIGHTS:N_MIX_WEIGHTS + 6]
    (hist_ref, carry_ref, ycat_ref, hid_ref, xm_ref, z_ref, bu_ref, xs_ref, hb2_ref,
     ypre_ref) = refs[N_MIX_WEIGHTS + 6:N_MIX_WEIGHTS + 16]
    i = pl.program_id(0)
    rows, d = xm_ref.shape
    steps = rows // SUBLANES
    gw = d // N_MIXERS
    hrows = HIST_STEPS * SUBLANES
    n_state = lre_ref.shape[1]
    d_ff = wg_ref.shape[1]

    @pl.when(i == 0)
    def _():
        hist_ref[...] = jnp.zeros(hist_ref.shape, F32)
        carry_ref[...] = jnp.zeros(carry_ref.shape, F32)
        ypre_ref[...] = jnp.zeros(ypre_ref.shape, F32)

    def ffn_chunk(k):
        def run():
            c0 = k * FFN_CHUNK
            hb2 = hb2_ref[...]
            g = jnp.dot(hb2, wg_ref[:, c0:c0 + FFN_CHUNK], preferred_element_type=F32)
            v = jnp.dot(hb2, wu_ref[:, c0:c0 + FFN_CHUNK], preferred_element_type=F32)
            hid_ref[:, c0:c0 + FFN_CHUNK] = (g * _sigmoid(g) * v).astype(BF16)
        return run

    st = {}

    def ffn_down():
        st["f"] = jnp.dot(hid_ref[...], wd_ref[...], preferred_element_type=F32)

    def ffn_norm():
        g2 = mod_ref[:, 5 * d:6 * d]
        x_prev = xm_ref[...]
        out = _layer_norm(alpha * x_prev + _per_batch(x_prev, 1.0 + g2) * st["f"],
                          ln2_g_ref[...], ln2_b_ref[...])
        if out_batch_major:
            out = jnp.swapaxes(out.reshape(steps, SUBLANES, d), 0, 1)
        o_ref[...] = out

    if x_batch_major:
        xt_ref = refs[N_MIX_WEIGHTS + 16]

        def load_x():
            xt_ref[...] = jnp.swapaxes(x_ref[...], 0, 1).reshape(rows, d)
    else:
        xt_ref = x_ref

        def load_x():
            pass

    def mix_front():
        x = xt_ref[...]
        sh1, sc1 = mod_ref[:, 0:d], mod_ref[:, d:2 * d]
        hb = (x * _per_batch(x, 1.0 + sc1) + _per_batch(x, sh1)).astype(BF16)
        z_ref[...] = jnp.dot(hb, w_in_ref[...], preferred_element_type=F32) + b_in_ref[...]
        hist_ref[0:hrows, :] = hist_ref[rows:rows + hrows, :]

    def ssm_in():
        bu_ref[...] = jnp.dot(z_ref[:, 6 * gw:7 * gw].astype(BF16), bcat_ref[...],
                              preferred_element_type=F32)

    def short_conv():
        z_h, z_b, z_c = z_ref[:, 0:gw], z_ref[:, gw:2 * gw], z_ref[:, 2 * gw:3 * gw]
        p = z_c * z_h
        hist_ref[hrows:hrows + rows, 0:gw] = p
        conv = sc_w_ref[SC_TAPS - 1:SC_TAPS, :] * p
        for k in range(SC_TAPS - 1):
            off = hrows - SUBLANES * (SC_TAPS - 1 - k)
            conv = conv + sc_w_ref[k:k + 1, :] * hist_ref[off:off + rows, 0:gw]
        ycat_ref[:, 0:gw] = (z_b * conv).astype(BF16)

    def pool():
        z_p = z_ref[:, 3 * gw:4 * gw]
        hist_ref[hrows:hrows + rows, gw:2 * gw] = z_p
        back = SUBLANES * (max(POOL_WINDOWS) - 1)
        run = hist_ref[hrows - back:hrows + rows, gw:2 * gw]
        lane = lax.broadcasted_iota(jnp.int32, (rows, gw), 1)
        pool_group = gw // len(POOL_WINDOWS)
        t_glob = i * steps + lax.broadcasted_iota(jnp.int32, (rows, gw), 0) // SUBLANES
        width = 1
        s_sel = None
        w_lane = None
        for gi, w in enumerate(POOL_WINDOWS):
            while width < w:
                sh = SUBLANES * width
                run = run[sh:] + run[:-sh]
                width *= 2
            cur = run[run.shape[0] - rows:]
            if s_sel is None:
                s_sel, w_lane = cur, jnp.full((rows, gw), float(w), F32)
            else:
                in_group = lane >= gi * pool_group
                s_sel = jnp.where(in_group, cur, s_sel)
                w_lane = jnp.where(in_group, float(w), w_lane)
        cnt = jnp.minimum((t_glob + 1).astype(F32), w_lane)
        ycat_ref[:, gw:2 * gw] = (s_sel / cnt - z_p).astype(BF16)

    def pool_mix():
        y_b = jnp.dot(ycat_ref[:, gw:2 * gw], wp_ref[...], preferred_element_type=F32) * pscale_ref[...]
        ycat_ref[:, gw:2 * gw] = y_b.astype(BF16)

    def cf_gate():
        hist_ref[hrows:hrows + rows, 2 * gw:3 * gw] = (
            z_ref[:, 4 * gw:5 * gw] * _sigmoid(z_ref[:, 5 * gw:6 * gw]))

    crow = rows // CONV_BLOCKS

    def cf_conv(blk):
        def run():
            r0 = blk * crow
            acc = jnp.broadcast_to(cfb_ref[...], (crow, gw))
            for k in range(CF_TAPS):
                off = hrows - SUBLANES * (CF_TAPS - 1 - k) + r0
                acc = acc + cfw_ref[k:k + 1, :] * hist_ref[off:off + crow, 2 * gw:3 * gw]
            hn = _layer_norm(acc, cfg_ref[...], cfbeta_ref[...])
            ycat_ref[r0:r0 + crow, 2 * gw:3 * gw] = (hn * _sigmoid(hn)).astype(BF16)
        return run

    gsteps = steps // SCAN_GROUPS

    def scan(grp):
        def run():
            a_re, a_im = lre_ref[...], lim_ref[...]
            if grp == 0:
                re, im = carry_ref[:, 0:n_state], carry_ref[:, n_state:2 * n_state]
            else:
                re, im = st["re"], st["im"]
            for s in range(grp * gsteps, (grp + 1) * gsteps, 2):
                res, ims = [], []
                for q in range(2):
                    r0 = (s + q) * SUBLANES
                    b_re = bu_ref[r0:r0 + SUBLANES, 0:n_state]
                    b_im = bu_ref[r0:r0 + SUBLANES, n_state:2 * n_state]
                    re, im = a_re * re - a_im * im + b_re, a_re * im + a_im * re + b_im
                    res.append(re)
                    ims.append(im)
                r0 = s * SUBLANES
                xs_ref[r0:r0 + 2 * SUBLANES, 0:n_state] = jnp.concatenate(res, axis=0).astype(BF16)
                xs_ref[r0:r0 + 2 * SUBLANES, n_state:2 * n_state] = jnp.concatenate(ims, axis=0).astype(BF16)
            st["re"], st["im"] = re, im
            if grp == SCAN_GROUPS - 1:
                carry_ref[:, 0:n_state] = re
                carry_ref[:, n_state:2 * n_state] = im
        return run

    def ssm_out():
        u = z_ref[:, 6 * gw:7 * gw]
        st["y_s"] = jnp.dot(xs_ref[...], ccat_ref[...], preferred_element_type=F32) + dskip_ref[...] * u

    def ssm_glu():
        y_g = jax.nn.gelu(st["y_s"])
        gate = jnp.dot(y_g.astype(BF16), wglu_ref[...], preferred_element_type=F32) + bglu_ref[...]
        ycat_ref[:, 3 * gw:4 * gw] = (y_g * _sigmoid(gate)).astype(BF16)

    def mix_out():
        st["y"] = jnp.dot(ycat_ref[...], wo_ref[...], preferred_element_type=F32)

    def mix_residual():
        x = xt_ref[...]
        g1 = mod_ref[:, 2 * d:3 * d]
        ypre_ref[...] = alpha * x + _per_batch(x, 1.0 + g1) * st["y"]

    def prev_norm():
        x_mid = _layer_norm(ypre_ref[...], ln1_g_ref[...], ln1_b_ref[...])
        xm_ref[...] = x_mid
        sh2, sc2 = mod_ref[:, 3 * d:4 * d], mod_ref[:, 4 * d:5 * d]
        hb2_ref[...] = (x_mid * _per_batch(x_mid, 1.0 + sc2) + _per_batch(x_mid, sh2)).astype(BF16)

    def both(*fs):
        def run():
            for f in fs:
                f()
        return run

    mixer_pieces = ([both(mix_front, prev_norm), both(short_conv, pool), both(cf_gate, ssm_in)]
                    + [cf_conv(b) for b in range(CONV_BLOCKS)] + [scan(g) for g in range(SCAN_GROUPS)])
    n_chunks = d_ff // FFN_CHUNK
    assert len(mixer_pieces) == n_chunks, (len(mixer_pieces), n_chunks)
    order = [load_x, mixer_pieces[0]]
    for k in range(n_chunks):
        order += [ffn_chunk(k)] + ([mixer_pieces[k + 1]] if k + 1 < n_chunks else [])
    order += [pool_mix, ssm_out, ffn_down, ssm_glu, mix_out, ffn_norm, mix_residual]
    for piece in order:
        piece()


def _layer_spec(stacked_shape, layer):
    tail = len(stacked_shape) - 1
    return pl.BlockSpec((None,) + tuple(stacked_shape[1:]), lambda i: (layer,) + (0,) * tail,
                        pipeline_mode=pl.Buffered(1))


def _layer(xt, mod, mix_weights, ffn_weights, *, layer, alpha, x_batch_major, out_batch_major):
    if x_batch_major:
        bsz, seq, d = xt.shape
        n_rows = bsz * seq
    else:
        n_rows, d = xt.shape
    rows = TILE_STEPS * SUBLANES
    hrows = HIST_STEPS * SUBLANES
    gw = d // N_MIXERS
    n_tiles = n_rows // rows
    n_state2 = mix_weights[11].shape[2]
    d_ff = ffn_weights[0].shape[2]
    weights = list(mix_weights) + list(ffn_weights)
    assert len(mix_weights) == N_MIX_WEIGHTS
    if x_batch_major:
        x_spec = pl.BlockSpec((SUBLANES, TILE_STEPS, d), lambda i: (0, jnp.minimum(i, n_tiles - 1), 0))
    else:
        x_spec = pl.BlockSpec((rows, d), lambda i: (jnp.minimum(i, n_tiles - 1), 0))
    if out_batch_major:
        out_spec = pl.BlockSpec((SUBLANES, TILE_STEPS, d), lambda i: (0, jnp.maximum(i - 1, 0), 0))
        out_shape = jax.ShapeDtypeStruct((SUBLANES, n_rows // SUBLANES, d), F32)
    else:
        out_spec = pl.BlockSpec((rows, d), lambda i: (jnp.maximum(i - 1, 0), 0))
        out_shape = jax.ShapeDtypeStruct((n_rows, d), F32)
    extra_scratch = [pltpu.VMEM((rows, d), F32)] if x_batch_major else []
    return pl.pallas_call(
        functools.partial(_layer_kernel, alpha=alpha, x_batch_major=x_batch_major,
                          out_batch_major=out_batch_major),
        grid=(n_tiles + 1,),
        in_specs=[x_spec, _layer_spec(mod.shape, layer)] + [_layer_spec(w.shape, layer) for w in weights],
        out_specs=out_spec,
        out_shape=out_shape,
        scratch_shapes=[
            pltpu.VMEM((hrows + rows, 3 * gw), F32),
            pltpu.VMEM((SUBLANES, n_state2), F32),
            pltpu.VMEM((rows, d), BF16),
            pltpu.VMEM((rows, d_ff), BF16),
            pltpu.VMEM((rows, d), F32),
            pltpu.VMEM((rows, 7 * gw), F32),
            pltpu.VMEM((rows, n_state2), F32),
            pltpu.VMEM((rows, n_state2), BF16),
            pltpu.VMEM((rows, d), BF16),
            pltpu.VMEM((rows, d), F32),
        ] + extra_scratch,
        compiler_params=pltpu.CompilerParams(
            dimension_semantics=("arbitrary",), vmem_limit_bytes=VMEM_LIMIT_BYTES, flags=LAYER_FLAGS),
        name="decoder_layer",
    )(xt, mod, *weights)


def _block_diag(blocks):
    depth, n, a, b = blocks.shape
    eye = jnp.eye(n, dtype=blocks.dtype)
    return jnp.einsum("lnab,nm->lnamb", blocks, eye).reshape(depth, n * a, n * b)


def _ssm_tables(lam_re, lam_im, log_dt, b_re, b_im, c_re, c_im):
    depth, groups, n_p = lam_re.shape
    dt = jnp.exp(log_dt)[..., None]
    mag = jnp.exp(lam_re * dt)
    bar_re = mag * jnp.cos(lam_im * dt)
    bar_im = mag * jnp.sin(lam_im * dt)
    nr, ni = bar_re - 1.0, bar_im
    den = lam_re * lam_re + lam_im * lam_im
    q_re = (nr * lam_re + ni * lam_im) / den
    q_im = (ni * lam_re - nr * lam_im) / den
    bb_re = q_re[..., None] * b_re - q_im[..., None] * b_im
    bb_im = q_re[..., None] * b_im + q_im[..., None] * b_re
    bcat = jnp.concatenate([_block_diag(jnp.swapaxes(bb_re, 2, 3)), _block_diag(jnp.swapaxes(bb_im, 2, 3))],
                           axis=2)
    ccat = jnp.concatenate([_block_diag(jnp.swapaxes(c_re, 2, 3)), -_block_diag(jnp.swapaxes(c_im, 2, 3))],
                           axis=1)
    flat = (depth, 1, groups * n_p)
    lre = jnp.broadcast_to(bar_re.reshape(flat), (depth, SUBLANES, groups * n_p))
    lim = jnp.broadcast_to(bar_im.reshape(flat), (depth, SUBLANES, groups * n_p))
    return lre, lim, bcat.astype(BF16), ccat.astype(BF16)


def kernel(x, c, w_ada, b_ada, w_in, b_in, sc_w, pool_w, pool_scale, cf_dw_w, cf_dw_b, cf_ln_g, cf_ln_b,
           ssm_lam_re, ssm_lam_im, ssm_log_dt, ssm_b_re, ssm_b_im, ssm_c_re, ssm_c_im, ssm_d,
           ssm_w_glu, ssm_b_glu, w_o, ln1_g, ln1_b, w_gate, w_up, w_down, ln2_g, ln2_b):
    bsz, seq, d = x.shape
    depth = w_in.shape[0]
    assert bsz == SUBLANES, "layout puts the batch on the 8 sublanes"
    assert seq % TILE_STEPS == 0 and HIST_STEPS >= CF_TAPS - 1 and HIST_STEPS <= TILE_STEPS
    alpha = (2 * depth) ** 0.25

    mod = _modulation(c, w_ada, b_ada)
    lre, lim, bcat, ccat = _ssm_tables(ssm_lam_re, ssm_lam_im, ssm_log_dt, ssm_b_re, ssm_b_im,
                                       ssm_c_re, ssm_c_im)
    wp = _block_diag(pool_w).astype(BF16)
    row = lambda a: a.reshape(depth, 1, a.shape[-1])
    w_in_b, w_o_b, w_glu_b = w_in.astype(BF16), w_o.astype(BF16), ssm_w_glu.astype(BF16)
    wg_b, wu_b, wd_b = w_gate.astype(BF16), w_up.astype(BF16), w_down.astype(BF16)

    xt = x
    mix_weights = (w_in_b, row(b_in), sc_w, wp, row(pool_scale), cf_dw_w, row(cf_dw_b), row(cf_ln_g),
                   row(cf_ln_b), lre, lim, bcat, ccat, row(ssm_d), w_glu_b, row(ssm_b_glu), w_o_b,
                   row(ln1_g), row(ln1_b))
    ffn_weights = (wg_b, wu_b, wd_b, row(ln2_g), row(ln2_b))
    for l in range(depth):
        xt = _layer(xt, mod, mix_weights, ffn_weights, layer=l, alpha=alpha, x_batch_major=(l == 0),
                    out_batch_major=(l == depth - 1))
    return xt
```

```python
import functools

import jax
import jax.numpy as jnp
from jax import lax
from jax.experimental import pallas as pl
from jax.experimental.pallas import tpu as pltpu

F32 = jnp.float32
BF16 = jnp.bfloat16

N_MIXERS = 4
SC_TAPS = 3
POOL_WINDOWS = (2, 4, 8, 16)
CF_TAPS = 31
LN_EPS = 1e-5

SUBLANES = 8
VMEM_LIMIT_BYTES = 60 * 1024 * 1024

TILE_STEPS = 64
HIST_STEPS = 32
FFN_CHUNK = 256
MOD_TN = 1536
CONV_BLOCKS = 4
SCAN_GROUPS = 4

LAYER_FLAGS = {}


def _sigmoid(v):
    return 1.0 / (1.0 + jnp.exp(-v))


def _layer_norm(v, g, b):
    mu = jnp.mean(v, axis=-1, keepdims=True)
    c = v - mu
    var = jnp.mean(c * c, axis=-1, keepdims=True)
    return c * lax.rsqrt(var + LN_EPS) * g + b


def _per_batch(v, m):
    rows, d = v.shape
    return jnp.broadcast_to(m[None], (rows // SUBLANES, SUBLANES, d)).reshape(rows, d)


def _mod_kernel(c_ref, w_ref, b_ref, o_ref):
    c = c_ref[...]
    cond = c * _sigmoid(c)
    o_ref[0] = jnp.dot(cond.astype(BF16), w_ref[0].astype(BF16), preferred_element_type=F32) + b_ref[0]


def _modulation(c, w_ada, b_ada):
    depth, d, n = w_ada.shape
    bsz = c.shape[0]
    return pl.pallas_call(
        _mod_kernel,
        grid=(depth, n // MOD_TN),
        in_specs=[
            pl.BlockSpec((bsz, d), lambda l, j: (0, 0)),
            pl.BlockSpec((1, d, MOD_TN), lambda l, j: (l, 0, j)),
            pl.BlockSpec((1, 1, MOD_TN), lambda l, j: (l, 0, j)),
        ],
        out_specs=pl.BlockSpec((1, bsz, MOD_TN), lambda l, j: (l, 0, j)),
        out_shape=jax.ShapeDtypeStruct((depth, bsz, n), F32),
        compiler_params=pltpu.CompilerParams(
            dimension_semantics=("arbitrary", "arbitrary"), vmem_limit_bytes=VMEM_LIMIT_BYTES),
        name="adaln_mod",
    )(c, w_ada, b_ada.reshape(depth, 1, n))


N_MIX_WEIGHTS = 19


def _layer_kernel(x_ref, mod_ref, *refs, alpha, x_batch_major, out_batch_major):
    (w_in_ref, b_in_ref, sc_w_ref, wp_ref, pscale_ref, cfw_ref, cfb_ref, cfg_ref, cfbeta_ref, lre_ref,
     lim_ref, bcat_ref, ccat_ref, dskip_ref, wglu_ref, bglu_ref, wo_ref, ln1_g_ref,
     ln1_b_ref) = refs[:N_MIX_WEIGHTS]
    wg_ref, wu_ref, wd_ref, ln2_g_ref, ln2_b_ref, o_ref = refs[N_MIX_WEIGHTS:N_MIX_WEIGHTS + 6]
    (hist_ref, carry_ref, ycat_ref, hid_ref, xm_ref, z_ref, bu_ref, xs_ref, hb2_ref,
     ypre_ref) = refs[N_MIX_WEIGHTS + 6:N_MIX_WEIGHTS + 16]
    i = pl.program_id(0)
    rows, d = xm_ref.shape
    steps = rows // SUBLANES
    gw = d // N_MIXERS
    hrows = HIST_STEPS * SUBLANES
    n_state = lre_ref.shape[1]
    d_ff = wg_ref.shape[1]

    @pl.when(i == 0)
    def _():
        hist_ref[...] = jnp.zeros(hist_ref.shape, F32)
        carry_ref[...] = jnp.zeros(carry_ref.shape, F32)
        ypre_ref[...] = jnp.zeros(ypre_ref.shape, F32)

    def ffn_chunk(k):
        def run():
            c0 = k * FFN_CHUNK
            hb2 = hb2_ref[...]
            g = jnp.dot(hb2, wg_ref[:, c0:c0 + FFN_CHUNK], preferred_element_type=F32)
            v = jnp.dot(hb2, wu_ref[:, c0:c0 + FFN_CHUNK], preferred_element_type=F32)
            hid_ref[:, c0:c0 + FFN_CHUNK] = (g * _sigmoid(g) * v).astype(BF16)
            spare = pl.multiple_of(hrows + rows + jnp.minimum(i, 0), SUBLANES)
            hist_ref[pl.ds(spare, SUBLANES), 2 * gw:3 * gw] = g[0:SUBLANES, 0:gw]
        return run

    st = {}

    def ffn_down():
        st["f"] = jnp.dot(hid_ref[...], wd_ref[...], preferred_element_type=F32)

    def ffn_norm():
        g2 = mod_ref[:, 5 * d:6 * d]
        x_prev = xm_ref[...]
        out = _layer_norm(alpha * x_prev + _per_batch(x_prev, 1.0 + g2) * st["f"],
                          ln2_g_ref[...], ln2_b_ref[...])
        if out_batch_major:
            out = jnp.swapaxes(out.reshape(steps, SUBLANES, d), 0, 1)
        o_ref[...] = out

    if x_batch_major:
        xt_ref = refs[N_MIX_WEIGHTS + 16]

        def load_x():
            xt_ref[...] = jnp.swapaxes(x_ref[...], 0, 1).reshape(rows, d)
    else:
        xt_ref = x_ref

        def load_x():
            pass

    def mix_front():
        x = xt_ref[...]
        sh1, sc1 = mod_ref[:, 0:d], mod_ref[:, d:2 * d]
        hb = (x * _per_batch(x, 1.0 + sc1) + _per_batch(x, sh1)).astype(BF16)
        z_ref[...] = jnp.dot(hb, w_in_ref[...], preferred_element_type=F32) + b_in_ref[...]
        hist_ref[0:hrows, :] = hist_ref[rows:rows + hrows, :]

    def ssm_in():
        bu_ref[...] = jnp.dot(z_ref[:, 6 * gw:7 * gw].astype(BF16), bcat_ref[...],
                              preferred_element_type=F32)

    def short_conv():
        z_h, z_b, z_c = z_ref[:, 0:gw], z_ref[:, gw:2 * gw], z_ref[:, 2 * gw:3 * gw]
        p = z_c * z_h
        hist_ref[hrows:hrows + rows, 0:gw] = p
        conv = sc_w_ref[SC_TAPS - 1:SC_TAPS, :] * p
        for k in range(SC_TAPS - 1):
            off = hrows - SUBLANES * (SC_TAPS - 1 - k)
            conv = conv + sc_w_ref[k:k + 1, :] * hist_ref[off:off + rows, 0:gw]
        ycat_ref[:, 0:gw] = (z_b * conv).astype(BF16)

    def pool():
        z_p = z_ref[:, 3 * gw:4 * gw]
        hist_ref[hrows:hrows + rows, gw:2 * gw] = z_p
        back = SUBLANES * (max(POOL_WINDOWS) - 1)
        run = hist_ref[hrows - back:hrows + rows, gw:2 * gw]
        lane = lax.broadcasted_iota(jnp.int32, (rows, gw), 1)
        pool_group = gw // len(POOL_WINDOWS)
        t_glob = i * steps + lax.broadcasted_iota(jnp.int32, (rows, gw), 0) // SUBLANES
        width = 1
        s_sel = None
        w_lane = None
        for gi, w in enumerate(POOL_WINDOWS):
            while width < w:
                sh = SUBLANES * width
                run = run[sh:] + run[:-sh]
                width *= 2
            cur = run[run.shape[0] - rows:]
            if s_sel is None:
                s_sel, w_lane = cur, jnp.full((rows, gw), float(w), F32)
            else:
                in_group = lane >= gi * pool_group
                s_sel = jnp.where(in_group, cur, s_sel)
                w_lane = jnp.where(in_group, float(w), w_lane)
        cnt = jnp.minimum((t_glob + 1).astype(F32), w_lane)
        ycat_ref[:, gw:2 * gw] = (s_sel / cnt - z_p).astype(BF16)

    def pool_mix():
        y_b = jnp.dot(ycat_ref[:, gw:2 * gw], wp_ref[...], preferred_element_type=F32) * pscale_ref[...]
        ycat_ref[:, gw:2 * gw] = y_b.astype(BF16)

    def cf_gate():
        hist_ref[hrows:hrows + rows, 2 * gw:3 * gw] = (
            z_ref[:, 4 * gw:5 * gw] * _sigmoid(z_ref[:, 5 * gw:6 * gw]))

    crow = rows // CONV_BLOCKS

    def cf_conv(blk):
        def run():
            r0 = blk * crow
            acc = jnp.broadcast_to(cfb_ref[...], (crow, gw))
            for k in range(CF_TAPS):
                off = pl.multiple_of(hrows - SUBLANES * (CF_TAPS - 1 - k) + r0 + jnp.minimum(i, 0), SUBLANES)
                acc = acc + cfw_ref[k:k + 1, :] * hist_ref[pl.ds(off, crow), 2 * gw:3 * gw]
            hn = _layer_norm(acc, cfg_ref[...], cfbeta_ref[...])
            ycat_ref[r0:r0 + crow, 2 * gw:3 * gw] = (hn * _sigmoid(hn)).astype(BF16)
        return run

    gsteps = steps // SCAN_GROUPS

    def scan(grp):
        def run():
            a_re, a_im = lre_ref[...], lim_ref[...]
            if grp == 0:
                re, im = carry_ref[:, 0:n_state], carry_ref[:, n_state:2 * n_state]
            else:
                re, im = st["re"], st["im"]
            for s in range(grp * gsteps, (grp + 1) * gsteps, 2):
                res, ims = [], []
                for q in range(2):
                    r0 = (s + q) * SUBLANES
                    b_re = bu_ref[r0:r0 + SUBLANES, 0:n_state]
                    b_im = bu_ref[r0:r0 + SUBLANES, n_state:2 * n_state]
                    re, im = a_re * re - a_im * im + b_re, a_re * im + a_im * re + b_im
                    res.append(re)
                    ims.append(im)
                r0 = s * SUBLANES
                xs_ref[r0:r0 + 2 * SUBLANES, 0:n_state] = jnp.concatenate(res, axis=0).astype(BF16)
                xs_ref[r0:r0 + 2 * SUBLANES, n_state:2 * n_state] = jnp.concatenate(ims, axis=0).astype(BF16)
            st["re"], st["im"] = re, im
            if grp == SCAN_GROUPS - 1:
                carry_ref[:, 0:n_state] = re
                carry_ref[:, n_state:2 * n_state] = im
        return run

    def ssm_out():
        u = z_ref[:, 6 * gw:7 * gw]
        st["y_s"] = jnp.dot(xs_ref[...], ccat_ref[...], preferred_element_type=F32) + dskip_ref[...] * u

    def ssm_glu():
        y_g = jax.nn.gelu(st["y_s"])
        gate = jnp.dot(y_g.astype(BF16), wglu_ref[...], preferred_element_type=F32) + bglu_ref[...]
        ycat_ref[:, 3 * gw:4 * gw] = (y_g * _sigmoid(gate)).astype(BF16)

    def mix_out():
        st["y"] = jnp.dot(ycat_ref[...], wo_ref[...], preferred_element_type=F32)

    def mix_residual():
        x = xt_ref[...]
        g1 = mod_ref[:, 2 * d:3 * d]
        ypre_ref[...] = alpha * x + _per_batch(x, 1.0 + g1) * st["y"]

    def prev_norm():
        x_mid = _layer_norm(ypre_ref[...], ln1_g_ref[...], ln1_b_ref[...])
        xm_ref[...] = x_mid
        sh2, sc2 = mod_ref[:, 3 * d:4 * d], mod_ref[:, 4 * d:5 * d]
        hb2_ref[...] = (x_mid * _per_batch(x_mid, 1.0 + sc2) + _per_batch(x_mid, sh2)).astype(BF16)

    def both(*fs):
        def run():
            for f in fs:
                f()
        return run

    mixer_pieces = ([both(mix_front, prev_norm), both(short_conv, pool), both(cf_gate, ssm_in)]
                    + [cf_conv(b) for b in range(CONV_BLOCKS)] + [scan(g) for g in range(SCAN_GROUPS)])
    n_chunks = d_ff // FFN_CHUNK
    assert len(mixer_pieces) == n_chunks, (len(mixer_pieces), n_chunks)
    order = [load_x, mixer_pieces[0]]
    for k in range(n_chunks):
        order += [ffn_chunk(k)] + ([mixer_pieces[k + 1]] if k + 1 < n_chunks else [])
    order += [pool_mix, ssm_out, ffn_down, ssm_glu, mix_out, ffn_norm, mix_residual]
    for piece in order:
        piece()


def _layer_spec(stacked_shape, layer):
    tail = len(stacked_shape) - 1
    return pl.BlockSpec((None,) + tuple(stacked_shape[1:]), lambda i: (layer,) + (0,) * tail,
                        pipeline_mode=pl.Buffered(1))


def _layer(xt, mod, mix_weights, ffn_weights, *, layer, alpha, x_batch_major, out_batch_major):
    if x_batch_major:
        bsz, seq, d = xt.shape
        n_rows = bsz * seq
    else:
        n_rows, d = xt.shape
    rows = TILE_STEPS * SUBLANES
    hrows = HIST_STEPS * SUBLANES
    gw = d // N_MIXERS
    n_tiles = n_rows // rows
    n_state2 = mix_weights[11].shape[2]
    d_ff = ffn_weights[0].shape[2]
    weights = list(mix_weights) + list(ffn_weights)
    assert len(mix_weights) == N_MIX_WEIGHTS
    if x_batch_major:
        x_spec = pl.BlockSpec((SUBLANES, TILE_STEPS, d), lambda i: (0, jnp.minimum(i, n_tiles - 1), 0))
    else:
        x_spec = pl.BlockSpec((rows, d), lambda i: (jnp.minimum(i, n_tiles - 1), 0))
    if out_batch_major:
        out_spec = pl.BlockSpec((SUBLANES, TILE_STEPS, d), lambda i: (0, jnp.maximum(i - 1, 0), 0))
        out_shape = jax.ShapeDtypeStruct((SUBLANES, n_rows // SUBLANES, d), F32)
    else:
        out_spec = pl.BlockSpec((rows, d), lambda i: (jnp.maximum(i - 1, 0), 0))
        out_shape = jax.ShapeDtypeStruct((n_rows, d), F32)
    extra_scratch = [pltpu.VMEM((rows, d), F32)] if x_batch_major else []
    return pl.pallas_call(
        functools.partial(_layer_kernel, alpha=alpha, x_batch_major=x_batch_major,
                          out_batch_major=out_batch_major),
        grid=(n_tiles + 1,),
        in_specs=[x_spec, _layer_spec(mod.shape, layer)] + [_layer_spec(w.shape, layer) for w in weights],
        out_specs=out_spec,
        out_shape=out_shape,
        scratch_shapes=[
            pltpu.VMEM((hrows + rows + SUBLANES, 3 * gw), F32),
            pltpu.VMEM((SUBLANES, n_state2), F32),
            pltpu.VMEM((rows, d), BF16),
            pltpu.VMEM((rows, d_ff), BF16),
            pltpu.VMEM((rows, d), F32),
            pltpu.VMEM((rows, 7 * gw), F32),
            pltpu.VMEM((rows, n_state2), F32),
            pltpu.VMEM((rows, n_state2), BF16),
            pltpu.VMEM((rows, d), BF16),
            pltpu.VMEM((rows, d), F32),
        ] + extra_scratch,
        compiler_params=pltpu.CompilerParams(
            dimension_semantics=("arbitrary",), vmem_limit_bytes=VMEM_LIMIT_BYTES, flags=LAYER_FLAGS),
        name="decoder_layer",
    )(xt, mod, *weights)


def _block_diag(blocks):
    depth, n, a, b = blocks.shape
    eye = jnp.eye(n, dtype=blocks.dtype)
    return jnp.einsum("lnab,nm->lnamb", blocks, eye).reshape(depth, n * a, n * b)


def _ssm_tables(lam_re, lam_im, log_dt, b_re, b_im, c_re, c_im):
    depth, groups, n_p = lam_re.shape
    dt = jnp.exp(log_dt)[..., None]
    mag = jnp.exp(lam_re * dt)
    bar_re = mag * jnp.cos(lam_im * dt)
    bar_im = mag * jnp.sin(lam_im * dt)
    nr, ni = bar_re - 1.0, bar_im
    den = lam_re * lam_re + lam_im * lam_im
    q_re = (nr * lam_re + ni * lam_im) / den
    q_im = (ni * lam_re - nr * lam_im) / den
    bb_re = q_re[..., None] * b_re - q_im[..., None] * b_im
    bb_im = q_re[..., None] * b_im + q_im[..., None] * b_re
    bcat = jnp.concatenate([_block_diag(jnp.swapaxes(bb_re, 2, 3)), _block_diag(jnp.swapaxes(bb_im, 2, 3))],
                           axis=2)
    ccat = jnp.concatenate([_block_diag(jnp.swapaxes(c_re, 2, 3)), -_block_diag(jnp.swapaxes(c_im, 2, 3))],
                           axis=1)
    flat = (depth, 1, groups * n_p)
    lre = jnp.broadcast_to(bar_re.reshape(flat), (depth, SUBLANES, groups * n_p))
    lim = jnp.broadcast_to(bar_im.reshape(flat), (depth, SUBLANES, groups * n_p))
    return lre, lim, bcat.astype(BF16), ccat.astype(BF16)


def kernel(x, c, w_ada, b_ada, w_in, b_in, sc_w, pool_w, pool_scale, cf_dw_w, cf_dw_b, cf_ln_g, cf_ln_b,
           ssm_lam_re, ssm_lam_im, ssm_log_dt, ssm_b_re, ssm_b_im, ssm_c_re, ssm_c_im, ssm_d,
           ssm_w_glu, ssm_b_glu, w_o, ln1_g, ln1_b, w_gate, w_up, w_down, ln2_g, ln2_b):
    bsz, seq, d = x.shape
    depth = w_in.shape[0]
    assert bsz == SUBLANES, "layout puts the batch on the 8 sublanes"
    assert seq % TILE_STEPS == 0 and HIST_STEPS >= CF_TAPS - 1 and HIST_STEPS <= TILE_STEPS
    alpha = (2 * depth) ** 0.25

    mod = _modulation(c, w_ada, b_ada)
    lre, lim, bcat, ccat = _ssm_tables(ssm_lam_re, ssm_lam_im, ssm_log_dt, ssm_b_re, ssm_b_im,
                                       ssm_c_re, ssm_c_im)
    wp = _block_diag(pool_w).astype(BF16)
    row = lambda a: a.reshape(depth, 1, a.shape[-1])
    w_in_b, w_o_b, w_glu_b = w_in.astype(BF16), w_o.astype(BF16), ssm_w_glu.astype(BF16)
    wg_b, wu_b, wd_b = w_gate.astype(BF16), w_up.astype(BF16), w_down.astype(BF16)

    xt = x
    mix_weights = (w_in_b, row(b_in), sc_w, wp, row(pool_scale), cf_dw_w, row(cf_dw_b), row(cf_ln_g),
                   row(cf_ln_b), lre, lim, bcat, ccat, row(ssm_d), w_glu_b, row(ssm_b_glu), w_o_b,
                   row(ln1_g), row(ln1_b))
    ffn_weights = (wg_b, wu_b, wd_b, row(ln2_g), row(ln2_b))
    for l in range(depth):
        xt = _layer(xt, mod, mix_weights, ffn_weights, layer=l, alpha=alpha, x_batch_major=(l == 0),
                    out_batch_major=(l == depth - 1))
    return xt
```
